```python
import math
import jax, jax.numpy as jnp
from jax import lax
import numpy as np

D_MODEL = 1024
BATCH = 4
SEQ = 8192
DEPTH = 1

N_ATT_HEADS = 8
ATT_HEAD_DIM = 64
ATT_V_DIM = 2 * ATT_HEAD_DIM
D_ATT_QK = N_ATT_HEADS * 2 * ATT_HEAD_DIM
D_ATT_V = N_ATT_HEADS * ATT_V_DIM
Q_BLOCK = 128
D_RNN = D_MODEL
N_RNN_BLOCKS = 8
RNN_BLOCK = D_RNN // N_RNN_BLOCKS
CONV_WIDTH = 4
LRU_C = 8.0
N_BRANCH = 2
D_FF = 4 * D_MODEL
D_IN = 2 * D_ATT_QK + D_ATT_V + 2 * D_RNN + N_BRANCH * D_MODEL
EPS = 1e-6

kernel_name = 'hybrid_diffattn_rglru_gated_block'


def _rmsnorm(x, g):
    x32 = x.astype(jnp.float32)
    y = x32 * lax.rsqrt(jnp.mean(jnp.square(x32), axis=-1, keepdims=True) + EPS)
    return (y * g.astype(jnp.float32)).astype(x.dtype)


def _alibi_slopes(n):
    return 2.0 ** (-8.0 * jnp.arange(1, n + 1, dtype=jnp.float32) / n)


def _diff_attention(q, k, v, lam):
    b, s = q.shape[:2]
    nblk = s // Q_BLOCK
    slopes = _alibi_slopes(N_ATT_HEADS)
    kpos = jnp.arange(s)
    qb = q.reshape(b, nblk, Q_BLOCK, N_ATT_HEADS, 2, ATT_HEAD_DIM).transpose(1, 0, 2, 3, 4, 5)

    def one_block(args):
        i, qi = args
        qpos = i * Q_BLOCK + jnp.arange(Q_BLOCK)
        dist = (qpos[:, None] - kpos[None, :]).astype(jnp.float32)
        bias = jnp.where(dist >= 0, -slopes[:, None, None] * dist, -jnp.inf)
        scores = jnp.einsum('bqhcd,bkhcd->bhcqk', qi, k, preferred_element_type=jnp.float32)
        p = jax.nn.softmax(scores + bias[None, :, None], axis=-1)
        attn = (p[:, :, 0] - lam * p[:, :, 1]).astype(v.dtype)
        return jnp.einsum('bhqk,bkhe->bqhe', attn, v)

    out = lax.map(one_block, (jnp.arange(nblk), qb))
    return out.transpose(1, 0, 2, 3, 4).reshape(b, s, N_ATT_HEADS, ATT_V_DIM)


def _rglru_branch(xb, gb, conv_w, conv_b, w_r, b_r, w_i, b_i, lru_lambda):
    b, s, _ = xb.shape
    xp = jnp.pad(xb, ((0, 0), (CONV_WIDTH - 1, 0), (0, 0)))
    xc = conv_b + xp[:, 0:s] * conv_w[0]
    for j in range(1, CONV_WIDTH):
        xc = xc + xp[:, j:j + s] * conv_w[j]
    xr = xc.reshape(b, s, N_RNN_BLOCKS, RNN_BLOCK)
    r = jax.nn.sigmoid(jnp.einsum('bsnc,ncd->bsnd', xr, w_r).reshape(b, s, D_RNN) + b_r)
    i = jax.nn.sigmoid(jnp.einsum('bsnc,ncd->bsnd', xr, w_i).reshape(b, s, D_RNN) + b_i)
    log_a = -LRU_C * jax.nn.softplus(-lru_lambda.astype(jnp.float32)) * r.astype(jnp.float32)
    a = jnp.exp(log_a)
    mult = jnp.sqrt(-jnp.expm1(2.0 * log_a))
    pos = jnp.arange(s)[None, :, None]
    mult = jnp.where(pos == 0, 1.0, mult)
    u = mult * (i * xc).astype(jnp.float32)

    def combine(left, right):
        a1, b1 = left
        a2, b2 = right
        return a1 * a2, a2 * b1 + b2

    _, h = lax.associative_scan(combine, (a, u), axis=1)
    return h.astype(xb.dtype) * jax.nn.gelu(gb)


def setup_inputs(seed: int = 0) -> dict:
    key = jax.random.key(seed)
    ks = jax.random.split(key, 24)
    L = DEPTH
    nrm = lambda k, shp, sc: jax.random.normal(k, shp, jnp.float32) * sc
    u = jax.random.uniform(ks[11], (L, D_RNN), jnp.float32, 0.9, 0.999)
    a0 = u ** (1.0 / LRU_C)
    lru_lambda = jnp.log(a0) - jnp.log1p(-a0)
    return {
        'x': nrm(ks[0], (BATCH, SEQ, D_MODEL), 1.0),
        'w_in': nrm(ks[1], (L, D_MODEL, D_IN), D_MODEL ** -0.5),
        'b_gate': nrm(ks[2], (L, N_BRANCH * D_MODEL), 0.02),
        'g_mix': 1.0 + nrm(ks[3], (L, D_MODEL), 0.02),
        'lambda_q1': nrm(ks[4], (L, ATT_HEAD_DIM), 0.1),
        'lambda_k1': nrm(ks[5], (L, ATT_HEAD_DIM), 0.1),
        'lambda_q2': nrm(ks[6], (L, ATT_HEAD_DIM), 0.1),
        'lambda_k2': nrm(ks[7], (L, ATT_HEAD_DIM), 0.1),
        'subln_g': 1.0 + nrm(ks[8], (L, ATT_V_DIM), 0.02),
        'conv_w': nrm(ks[9], (L, CONV_WIDTH, D_RNN), CONV_WIDTH ** -0.5),
        'conv_b': nrm(ks[10], (L, D_RNN), 0.02),
        'w_r': nrm(ks[12], (L, N_RNN_BLOCKS, RNN_BLOCK, RNN_BLOCK), RNN_BLOCK ** -0.5),
        'b_r': nrm(ks[13], (L, D_RNN), 0.02),
        'w_i': nrm(ks[14], (L, N_RNN_BLOCKS, RNN_BLOCK, RNN_BLOCK), RNN_BLOCK ** -0.5),
        'b_i': nrm(ks[15], (L, D_RNN), 0.02),
        'lru_lambda': lru_lambda,
        'w_att_out': nrm(ks[16], (L, D_ATT_V, D_MODEL), D_ATT_V ** -0.5),
        'w_rnn_out': nrm(ks[17], (L, D_RNN, D_MODEL), D_RNN ** -0.5),
        'w_o': nrm(ks[18], (L, D_MODEL, D_MODEL), D_MODEL ** -0.5),
        'g_mlp': 1.0 + nrm(ks[19], (L, D_MODEL), 0.02),
        'w_ff1': nrm(ks[20], (L, D_MODEL, D_FF), D_MODEL ** -0.5),
        'w_ff2': nrm(ks[21], (L, D_FF, D_MODEL), D_FF ** -0.5),
        'g_final': 1.0 + nrm(ks[22], (D_MODEL,), 0.02),
    }


def reference(x, w_in, b_gate, g_mix, lambda_q1, lambda_k1, lambda_q2, lambda_k2, subln_g,
              conv_w, conv_b, w_r, b_r, w_i, b_i, lru_lambda, w_att_out, w_rnn_out, w_o,
              g_mlp, w_ff1, w_ff2, g_final):
    b, s, _ = x.shape
    splits = [D_ATT_QK, 2 * D_ATT_QK, 2 * D_ATT_QK + D_ATT_V,
              2 * D_ATT_QK + D_ATT_V + D_RNN, 2 * D_ATT_QK + D_ATT_V + 2 * D_RNN]
    for l in range(DEPTH):
        h = _rmsnorm(x, g_mix[l])
        z = h @ w_in[l]
        q, k, v, xb, gb, gates = jnp.split(z, splits, axis=-1)
        q = q.reshape(b, s, N_ATT_HEADS, 2, ATT_HEAD_DIM) * (ATT_HEAD_DIM ** -0.5)
        k = k.reshape(b, s, N_ATT_HEADS, 2, ATT_HEAD_DIM)
        v = v.reshape(b, s, N_ATT_HEADS, ATT_V_DIM)
        lam_init = 0.8 - 0.6 * math.exp(-0.3 * l)
        lam = (jnp.exp(jnp.sum(lambda_q1[l].astype(jnp.float32) * lambda_k1[l].astype(jnp.float32)))
               - jnp.exp(jnp.sum(lambda_q2[l].astype(jnp.float32) * lambda_k2[l].astype(jnp.float32)))
               + lam_init)
        att = _diff_attention(q, k, v, lam)
        att = _rmsnorm(att, subln_g[l]) * (1.0 - lam_init)
        y_att = att.reshape(b, s, D_ATT_V) @ w_att_out[l]
        y_rnn = _rglru_branch(xb, gb, conv_w[l], conv_b[l], w_r[l], b_r[l], w_i[l], b_i[l],
                              lru_lambda[l]) @ w_rnn_out[l]
        g = jax.nn.sigmoid(gates + b_gate[l]).reshape(b, s, N_BRANCH, D_MODEL)
        m = g[:, :, 0] * y_att + g[:, :, 1] * y_rnn
        x = x + m @ w_o[l]
        h2 = _rmsnorm(x, g_mlp[l])
        x = x + jnp.square(jax.nn.relu(h2 @ w_ff1[l])) @ w_ff2[l]
    return _rmsnorm(x, g_final)
```

```python
import functools
import math

import jax
import jax.numpy as jnp
from jax import lax
from jax.experimental import pallas as pl
from jax.experimental.pallas import tpu as pltpu

D_MODEL = 1024
N_HEADS = 8
HEAD_DIM = 64
V_DIM = 2 * HEAD_DIM
D_QK = N_HEADS * 2 * HEAD_DIM
D_V = N_HEADS * V_DIM
D_RNN = D_MODEL
N_RNN_BLOCKS = 8
RNN_BLOCK = D_RNN // N_RNN_BLOCKS
CONV_WIDTH = 4
LRU_C = 8.0
D_FF = 4 * D_MODEL
D_QKV = 2 * D_QK + D_V
D_REST = 2 * D_RNN + 2 * D_MODEL
EPS = 1e-6
LAM_INIT = 0.8 - 0.6 * math.exp(-0.3 * 0)
LOG2E = 1.4426950408889634
NEG_BIG = -1e30

ATT_TILE = 256
ATT_HEADS_PER_STEP = 2
PROJ_TM = 1024
PROJ_TN = 1024
RNN_TS = 256
FFN_TM = 256
VMEM_LIMIT = 56 * 1024 * 1024

F32 = jnp.float32
BF16 = jnp.bfloat16


def _rms(x, g):
    return x * lax.rsqrt(jnp.mean(x * x, axis=-1, keepdims=True) + EPS) * g


def _proj_kernel(x_ref, g_ref, w_ref, scale_ref, o_ref, h_sc):
    @pl.when(pl.program_id(1) == 0)
    def _():
        h_sc[...] = _rms(x_ref[...], g_ref[...]).astype(BF16)

    acc = jnp.dot(h_sc[...], w_ref[...], preferred_element_type=F32)
    o_ref[...] = (acc * scale_ref[...]).astype(o_ref.dtype)


def _proj(x2d, g, w, scale, out_dtype):
    t, d = x2d.shape
    n = w.shape[1]
    return pl.pallas_call(
        _proj_kernel,
        grid=(t // PROJ_TM, n // PROJ_TN),
        in_specs=[
            pl.BlockSpec((PROJ_TM, d), lambda i, j: (i, 0)),
            pl.BlockSpec((1, d), lambda i, j: (0, 0)),
            pl.BlockSpec((d, PROJ_TN), lambda i, j: (0, j)),
            pl.BlockSpec((1, PROJ_TN), lambda i, j: (0, j)),
        ],
        out_specs=pl.BlockSpec((PROJ_TM, PROJ_TN), lambda i, j: (i, j)),
        out_shape=jax.ShapeDtypeStruct((t, n), out_dtype),
        scratch_shapes=[pltpu.VMEM((PROJ_TM, d), BF16)],
        compiler_params=pltpu.CompilerParams(
            dimension_semantics=("parallel", "arbitrary"), vmem_limit_bytes=VMEM_LIMIT),
        name="proj",
    )(x2d, g, w, scale)


def _attn_kernel(lamp_ref, qT_ref, k_ref, vT_ref, bias_ref, biasd_ref, slope_ref, gsub_ref,
                 o_ref, qz_sc, m_sc, l_sc, acc_sc):
    g_heads = qT_ref.shape[1]
    tile = qT_ref.shape[3]
    i = pl.program_id(2)

    row = lax.broadcasted_iota(jnp.int32, (V_DIM, tile), 0)
    for g in range(g_heads):
        q_both = qT_ref[0, g].astype(F32)
        qz_sc[g, 0] = jnp.where(row < HEAD_DIM, q_both, 0.0).astype(BF16)
        qz_sc[g, 1] = jnp.where(row >= HEAD_DIM, q_both, 0.0).astype(BF16)
    m_sc[...] = jnp.full(m_sc.shape, NEG_BIG, F32)
    l_sc[...] = jnp.zeros(l_sc.shape, F32)
    acc_sc[...] = jnp.zeros(acc_sc.shape, F32)

    def step(j, b_ref):
        dist = jnp.full((1, tile), i - j, jnp.int32).astype(F32)
        k_rows = k_ref[0, pl.ds(pl.multiple_of(j * tile, tile), tile), :]
        for g in range(g_heads):
            kt = k_rows[:, g * V_DIM:(g + 1) * V_DIM]
            vt = vT_ref[0, g, j]
            bias = b_ref[g]
            dvec = slope_ref[g] * dist
            for c in range(2):
                s = jnp.dot(kt, qz_sc[g, c], preferred_element_type=F32)
                t = s + bias
                m_old = m_sc[g, c]
                m_new = jnp.maximum(m_old, jnp.max(t, axis=0, keepdims=True) - dvec)
                p = jnp.exp2(t - (m_new + dvec))
                alpha = jnp.exp2(m_old - m_new)
                l_sc[g, c] = alpha * l_sc[g, c] + jnp.sum(p, axis=0, keepdims=True)
                pv = jnp.dot(vt, p.astype(BF16), preferred_element_type=F32)
                acc_sc[g, c] = alpha * acc_sc[g, c] + pv
                m_sc[g, c] = m_new

    def body(j, carry):
        step(j, bias_ref)
        return carry

    lax.fori_loop(0, i, body, 0)
    step(i, biasd_ref)

    lp = lamp_ref[...]
    lam = (jnp.exp(jnp.sum(lp[0:1] * lp[1:2], axis=-1, keepdims=True))
           - jnp.exp(jnp.sum(lp[2:3] * lp[3:4], axis=-1, keepdims=True)) + LAM_INIT)
    for g in range(g_heads):
        o1 = acc_sc[g, 0] / l_sc[g, 0]
        o2 = acc_sc[g, 1] / l_sc[g, 1]
        a = o1 - lam * o2
        ms = jnp.mean(a * a, axis=0, keepdims=True)
        y = a * lax.rsqrt(ms + EPS) * gsub_ref[...] * (1.0 - LAM_INIT)
        o_ref[0, g] = y.astype(o_ref.dtype)


def _attention(lamp, qT, zqkv, vT5, bias, biasd, slope, gsub):
    b, h, _, s = qT.shape
    g = ATT_HEADS_PER_STEP
    t = ATT_TILE
    nq = s // t
    kblk0 = D_QK // (g * V_DIM)
    return pl.pallas_call(
        _attn_kernel,
        grid=(b, h // g, nq),
        in_specs=[
            pl.BlockSpec((4, HEAD_DIM), lambda bi, hi, qi: (0, 0)),
            pl.BlockSpec((1, g, V_DIM, t), lambda bi, hi, qi: (bi, hi, 0, qi)),
            pl.BlockSpec((1, s, g * V_DIM), lambda bi, hi, qi: (bi, 0, kblk0 + hi)),
            pl.BlockSpec((1, g, nq, V_DIM, t), lambda bi, hi, qi: (bi, hi, 0, 0, 0)),
            pl.BlockSpec((g, t, t), lambda bi, hi, qi: (hi, 0, 0)),
            pl.BlockSpec((g, t, t), lambda bi, hi, qi: (hi, 0, 0)),
            pl.BlockSpec((g, 1, t), lambda bi, hi, qi: (hi, 0, 0)),
            pl.BlockSpec((V_DIM, 1), lambda bi, hi, qi: (0, 0)),
        ],
        out_specs=pl.BlockSpec((1, g, V_DIM, t), lambda bi, hi, qi: (bi, hi, 0, qi)),
        out_shape=jax.ShapeDtypeStruct((b, h, V_DIM, s), BF16),
        scratch_shapes=[
            pltpu.VMEM((g, 2, V_DIM, t), BF16),
            pltpu.VMEM((g, 2, 1, t), F32),
            pltpu.VMEM((g, 2, 1, t), F32),
            pltpu.VMEM((g, 2, V_DIM, t), F32),
        ],
        compiler_params=pltpu.CompilerParams(
            dimension_semantics=("parallel", "parallel", "arbitrary"), vmem_limit_bytes=VMEM_LIMIT),
        name="attention",
    )(lamp, qT, zqkv, vT5, bias, biasd, slope, gsub)


def _rglru_kernel(xb_ref, gb_ref, cw_ref, cb_ref, wri_ref, br_ref, bi_ref, lam_ref,
                  o_ref, cx_sc, ch_sc, a_sc, u_sc):
    ts = xb_ref.shape[1]
    sblk = pl.program_id(1)

    @pl.when(sblk == 0)
    def _():
        cx_sc[...] = jnp.zeros(cx_sc.shape, F32)
        ch_sc[...] = jnp.zeros(ch_sc.shape, F32)

    x = xb_ref[0]
    row = lax.broadcasted_iota(jnp.int32, (ts, D_RNN), 0)
    row8 = lax.broadcasted_iota(jnp.int32, (8, D_RNN), 0)
    prev = cx_sc[...]
    cw = cw_ref[...]
    xc = cb_ref[...] + cw[CONV_WIDTH - 1:CONV_WIDTH] * x
    for d in range(1, CONV_WIDTH):
        xs = pltpu.roll(x, d, 0)
        head = jnp.where(row8 < d, pltpu.roll(prev, d, 0), xs[0:8])
        xs = jnp.concatenate([head, xs[8:]], axis=0)
        xc = xc + cw[CONV_WIDTH - 1 - d:CONV_WIDTH - d] * xs
    cx_sc[...] = x[ts - 8:ts]

    xcb = xc.astype(BF16)
    r_parts, i_parts = [], []
    for n in range(N_RNN_BLOCKS):
        ri = jnp.dot(xcb[:, n * RNN_BLOCK:(n + 1) * RNN_BLOCK], wri_ref[n], preferred_element_type=F32)
        r_parts.append(ri[:, :RNN_BLOCK])
        i_parts.append(ri[:, RNN_BLOCK:])
    r = jax.nn.sigmoid(jnp.concatenate(r_parts, axis=1) + br_ref[...])
    ig = jax.nn.sigmoid(jnp.concatenate(i_parts, axis=1) + bi_ref[...])

    nl = -lam_ref[...]
    sp = jnp.maximum(nl, 0.0) + jnp.log(1.0 + jnp.exp(-jnp.abs(nl)))
    log_a = (-LRU_C) * sp * r
    a = jnp.exp(log_a)
    mult = jnp.sqrt(1.0 - jnp.exp(2.0 * log_a))
    mult = jnp.where(jnp.logical_and(row == 0, sblk == 0), 1.0, mult)
    u = mult * (ig * xc)

    rm8 = jnp.bitwise_and(row, 7)
    for k in (1, 2, 4):
        keep = rm8 >= k
        a_sh = jnp.where(keep, pltpu.roll(a, k, 0), 1.0)
        u_sh = jnp.where(keep, pltpu.roll(u, k, 0), 0.0)
        u = a * u_sh + u
        a = a * a_sh
    a_sc[...] = a
    u_sc[...] = u

    def slab(rg, c):
        off = pl.multiple_of(rg * 8, 8)
        hblk = u_sc[pl.ds(off, 8), :] + a_sc[pl.ds(off, 8), :] * c
        u_sc[pl.ds(off, 8), :] = hblk
        return jnp.broadcast_to(hblk[7:8, :], (8, D_RNN))

    ch_sc[...] = lax.fori_loop(0, ts // 8, slab, ch_sc[...])
    o_ref[0] = (u_sc[...] * jax.nn.gelu(gb_ref[0])).astype(o_ref.dtype)


def _rglru(zrest, conv_w, conv_b, w_ri, b_r, b_i, lru_lambda):
    b, s, _ = zrest.shape
    ts = RNN_TS
    vec = lambda: pl.BlockSpec((1, D_RNN), lambda bi, si: (0, 0))
    return pl.pallas_call(
        _rglru_kernel,
        grid=(b, s // ts),
        in_specs=[
            pl.BlockSpec((1, ts, D_RNN), lambda bi, si: (bi, si, 0)),
            pl.BlockSpec((1, ts, D_RNN), lambda bi, si: (bi, si, 1)),
            pl.BlockSpec((CONV_WIDTH, D_RNN), lambda bi, si: (0, 0)),
            vec(),
            pl.BlockSpec((N_RNN_BLOCKS, RNN_BLOCK, 2 * RNN_BLOCK), lambda bi, si: (0, 0, 0)),
            vec(), vec(), vec(),
        ],
        out_specs=pl.BlockSpec((1, ts, D_RNN), lambda bi, si: (bi, si, 0)),
        out_shape=jax.ShapeDtypeStruct((b, s, D_RNN), BF16),
        scratch_shapes=[
            pltpu.VMEM((8, D_RNN), F32),
            pltpu.VMEM((8, D_RNN), F32),
            pltpu.VMEM((ts, D_RNN), F32),
            pltpu.VMEM((ts, D_RNN), F32),
        ],
        compiler_params=pltpu.CompilerParams(
            dimension_semantics=("parallel", "arbitrary"), vmem_limit_bytes=VMEM_LIMIT),
        name="rglru",
    )(zrest, zrest, conv_w, conv_b, w_ri, b_r, b_i, lru_lambda)


def _merge_ffn_kernel(x_ref, attT_ref, rnn_ref, gts_ref, bg_ref, wa_ref, wr_ref, wo_ref,
                      gm_ref, w1_ref, w2_ref, gf_ref, o_ref):
    x = x_ref[0]
    y_att = lax.dot_general(attT_ref[0], wa_ref[...], (((0,), (0,)), ((), ())),
                            preferred_element_type=F32)
    y_rnn = jnp.dot(rnn_ref[0], wr_ref[...], preferred_element_type=F32)
    gts = jax.nn.sigmoid(gts_ref[0] + bg_ref[...])
    m = gts[:, :D_MODEL] * y_att + gts[:, D_MODEL:] * y_rnn
    x1 = x + jnp.dot(m.astype(BF16), wo_ref[...], preferred_element_type=F32)
    h2 = _rms(x1, gm_ref[...]).astype(BF16)
    f = jnp.dot(h2, w1_ref[...], preferred_element_type=F32)
    f = jnp.square(jnp.maximum(f, 0.0)).astype(BF16)
    x2 = x1 + jnp.dot(f, w2_ref[...], preferred_element_type=F32)
    o_ref[0] = _rms(x2, gf_ref[...])


def _merge_ffn(x, attT, rnn, zrest, b_gate, w_att, w_rnn, w_o, g_mlp, w1, w2, g_final):
    b, s, d = x.shape
    tm = FFN_TM
    const = lambda shape: pl.BlockSpec(shape, lambda bi, si: (0,) * len(shape),
                                       pipeline_mode=pl.Buffered(1))
    return pl.pallas_call(
        _merge_ffn_kernel,
        grid=(b, s // tm),
        in_specs=[
            pl.BlockSpec((1, tm, d), lambda bi, si: (bi, si, 0)),
            pl.BlockSpec((1, D_V, tm), lambda bi, si: (bi, 0, si)),
            pl.BlockSpec((1, tm, D_RNN), lambda bi, si: (bi, si, 0)),
            pl.BlockSpec((1, tm, 2 * d), lambda bi, si: (bi, si, 1)),
            const((1, 2 * d)),
            const((D_V, d)), const((D_RNN, d)), const((d, d)),
            const((1, d)),
            const((d, D_FF)), const((D_FF, d)),
            const((1, d)),
        ],
        out_specs=pl.BlockSpec((1, tm, d), lambda bi, si: (bi, si, 0)),
        out_shape=jax.ShapeDtypeStruct((b, s, d), F32),
        compiler_params=pltpu.CompilerParams(
            dimension_semantics=("parallel", "parallel"), vmem_limit_bytes=VMEM_LIMIT),
        name="merge_ffn",
    )(x, attT, rnn, zrest, b_gate, w_att, w_rnn, w_o, g_mlp, w1, w2, g_final)


def _alibi_tables(tile):
    slopes = 2.0 ** (-8.0 * jnp.arange(1, N_HEADS + 1, dtype=F32) / N_HEADS) * LOG2E
    kr = jnp.arange(tile, dtype=F32)[:, None]
    qr = jnp.arange(tile, dtype=F32)[None, :]
    bias = slopes[:, None, None] * (kr - qr)[None]
    biasd = jnp.where((kr > qr)[None], NEG_BIG, bias)
    slope_rows = jnp.broadcast_to((slopes * tile)[:, None, None], (N_HEADS, 1, tile))
    return bias, biasd, slope_rows


def kernel(x, w_in, b_gate, g_mix, lambda_q1, lambda_k1, lambda_q2, lambda_k2, subln_g,
           conv_w, conv_b, w_r, b_r, w_i, b_i, lru_lambda, w_att_out, w_rnn_out, w_o,
           g_mlp, w_ff1, w_ff2, g_final):
    b, s, d = x.shape
    assert (d, w_in.shape[0]) == (D_MODEL, 1) and s % ATT_TILE == 0
    t = ATT_TILE
    x2d = x.reshape(b * s, d)
    w_in_b = w_in[0].astype(BF16)

    q_scale = jnp.concatenate([jnp.full((1, D_QK), HEAD_DIM ** -0.5 * LOG2E, F32),
                               jnp.ones((1, D_QKV - D_QK), F32)], axis=1)
    zqkv = _proj(x2d, g_mix, w_in_b[:, :D_QKV], q_scale, BF16).reshape(b, s, D_QKV)
    zrest = _proj(x2d, g_mix, w_in_b[:, D_QKV:], jnp.ones((1, D_REST), F32), F32).reshape(b, s, D_REST)

    qT = zqkv[:, :, :D_QK].reshape(b, s, N_HEADS, V_DIM).transpose(0, 2, 3, 1)
    vT5 = zqkv[:, :, 2 * D_QK:].reshape(b, s // t, t, N_HEADS, V_DIM).transpose(0, 3, 1, 4, 2)
    bias, biasd, slope_rows = _alibi_tables(t)
    lamp = jnp.concatenate([lambda_q1, lambda_k1, lambda_q2, lambda_k2], axis=0).astype(F32)
    attT = _attention(lamp, qT, zqkv, vT5, bias, biasd, slope_rows, subln_g.reshape(V_DIM, 1))
    attT = attT.reshape(b, D_V, s)

    w_ri = jnp.concatenate([w_r[0], w_i[0]], axis=-1).astype(BF16)
    rnn = _rglru(zrest, conv_w[0], conv_b, w_ri, b_r, b_i, lru_lambda)

    return _merge_ffn(x, attT, rnn, zrest, b_gate,
                      w_att_out[0].astype(BF16), w_rnn_out[0].astype(BF16), w_o[0].astype(BF16),
                      g_mlp, w_ff1[0].astype(BF16), w_ff2[0].astype(BF16), g_final.reshape(1, d))
```

```python
import functools
import math

import jax
import jax.numpy as jnp
from jax import lax
from jax.experimental import pallas as pl
from jax.experimental.pallas import tpu as pltpu

D_MODEL = 1024
N_HEADS = 8
HEAD_DIM = 64
V_DIM = 2 * HEAD_DIM
D_QK = N_HEADS * 2 * HEAD_DIM
D_V = N_HEADS * V_DIM
D_RNN = D_MODEL
N_RNN_BLOCKS = 8
RNN_BLOCK = D_RNN // N_RNN_BLOCKS
CONV_WIDTH = 4
LRU_C = 8.0
D_FF = 4 * D_MODEL
D_QKV = 2 * D_QK + D_V
D_REST = 2 * D_RNN + 2 * D_MODEL
EPS = 1e-6
LAM_INIT = 0.8 - 0.6 * math.exp(-0.3 * 0)
LOG2E = 1.4426950408889634
NEG_BIG = -1e30

ATT_TILE = 256
ATT_HEADS_PER_STEP = 2
PROJ_TM = 1024
PROJ_TN = 1024
RNN_TS = 256
FFN_TM = 256
VMEM_LIMIT = 56 * 1024 * 1024

F32 = jnp.float32
BF16 = jnp.bfloat16


def _rms(x, g):
    return x * lax.rsqrt(jnp.mean(x * x, axis=-1, keepdims=True) + EPS) * g


def _proj_kernel(x_ref, g_ref, w_ref, scale_ref, o_ref, h_sc):
    @pl.when(pl.program_id(1) == 0)
    def _():
        h_sc[...] = _rms(x_ref[...], g_ref[...]).astype(BF16)

    acc = jnp.dot(h_sc[...], w_ref[...], preferred_element_type=F32)
    o_ref[...] = (acc * scale_ref[...]).astype(o_ref.dtype)


def _proj(x2d, g, w, scale, out_dtype):
    t, d = x2d.shape
    n = w.shape[1]
    return pl.pallas_call(
        _proj_kernel,
        grid=(t // PROJ_TM, n // PROJ_TN),
        in_specs=[
            pl.BlockSpec((PROJ_TM, d), lambda i, j: (i, 0)),
            pl.BlockSpec((1, d), lambda i, j: (0, 0)),
            pl.BlockSpec((d, PROJ_TN), lambda i, j: (0, j)),
            pl.BlockSpec((1, PROJ_TN), lambda i, j: (0, j)),
        ],
        out_specs=pl.BlockSpec((PROJ_TM, PROJ_TN), lambda i, j: (i, j)),
        out_shape=jax.ShapeDtypeStruct((t, n), out_dtype),
        scratch_shapes=[pltpu.VMEM((PROJ_TM, d), BF16)],
        compiler_params=pltpu.CompilerParams(
            dimension_semantics=("parallel", "arbitrary"), vmem_limit_bytes=VMEM_LIMIT),
        name="proj",
    )(x2d, g, w, scale)


def _sublane_allmax(v):
    for sh in (4, 2, 1):
        v = jnp.maximum(v, pltpu.roll(v, sh, 0))
    return v


def _attn_kernel(lamp_ref, qT_ref, k_ref, vT_ref, bias_ref, biasd_ref, slope_ref, gsub_ref,
                 o_ref, qz_sc, s_sc, m_sc, l_sc, acc_sc):
    g_heads = qT_ref.shape[1]
    tile = qT_ref.shape[3]
    i = pl.program_id(2)
    chains = [(g, c) for g in range(g_heads) for c in range(2)]

    row = lax.broadcasted_iota(jnp.int32, (V_DIM, tile), 0)
    for g in range(g_heads):
        q_both = qT_ref[0, g].astype(F32)
        qz_sc[g, 0] = jnp.where(row < HEAD_DIM, q_both, 0.0).astype(BF16)
        qz_sc[g, 1] = jnp.where(row >= HEAD_DIM, q_both, 0.0).astype(BF16)
    m_sc[...] = jnp.full(m_sc.shape, NEG_BIG, F32)
    l_sc[...] = jnp.zeros(l_sc.shape, F32)
    acc_sc[...] = jnp.zeros(acc_sc.shape, F32)

    def scores(j, idx):
        g, c = chains[idx]
        kt = k_ref[0, pl.ds(pl.multiple_of(j * tile, tile), tile), g * V_DIM:(g + 1) * V_DIM]
        return jnp.dot(kt, qz_sc[g, c], preferred_element_type=F32)

    def update_acc(idx, alpha, pv):
        acc = acc_sc[idx].reshape(V_DIM // 8, 8, tile) * alpha[None] + pv.reshape(V_DIM // 8, 8, tile)
        acc_sc[idx] = acc.reshape(V_DIM, tile)

    def step(j, b_ref, prefetch_next):
        dist = jnp.full((8, tile), i - j, jnp.int32).astype(F32)
        pending = None
        for idx, (g, c) in enumerate(chains):
            s = s_sc[idx]
            s_next = scores(j + 1, idx) if prefetch_next else None
            dvec = slope_ref[g] * dist
            t = (s + b_ref[g]).reshape(tile // 8, 8, tile)
            m_old = m_sc[idx]
            m_new = jnp.maximum(m_old, _sublane_allmax(jnp.max(t, axis=0)) - dvec)
            p = jnp.exp2(t - (m_new + dvec)[None])
            alpha = jnp.exp2(m_old - m_new)
            l_sc[idx] = alpha * l_sc[idx] + jnp.sum(p, axis=0)
            m_sc[idx] = m_new
            pv = jnp.dot(vT_ref[0, g, j], p.reshape(tile, tile).astype(BF16),
                         preferred_element_type=F32)
            if prefetch_next:
                s_sc[idx] = s_next
            if pending is not None:
                update_acc(*pending)
            pending = (idx, alpha, pv)
        update_acc(*pending)

    for idx in range(len(chains)):
        s_sc[idx] = scores(0, idx)

    def body(j, carry):
        step(j, bias_ref, True)
        return carry

    lax.fori_loop(0, i, body, 0)
    step(i, biasd_ref, False)

    lp = lamp_ref[...]
    lam = (jnp.exp(jnp.sum(lp[0:1] * lp[1:2], axis=-1, keepdims=True))
           - jnp.exp(jnp.sum(lp[2:3] * lp[3:4], axis=-1, keepdims=True)) + LAM_INIT)
    for g in range(g_heads):
        l1 = jnp.sum(l_sc[2 * g], axis=0, keepdims=True)
        l2 = jnp.sum(l_sc[2 * g + 1], axis=0, keepdims=True)
        o1 = acc_sc[2 * g] / l1
        o2 = acc_sc[2 * g + 1] / l2
        a = o1 - lam * o2
        ms = jnp.mean(a * a, axis=0, keepdims=True)
        y = a * lax.rsqrt(ms + EPS) * gsub_ref[...] * (1.0 - LAM_INIT)
        o_ref[0, g] = y.astype(o_ref.dtype)


def _attention(lamp, qT, zqkv, vT5, bias, biasd, slope, gsub):
    b, h, _, s = qT.shape
    g = ATT_HEADS_PER_STEP
    t = ATT_TILE
    nq = s // t
    kblk0 = D_QK // (g * V_DIM)
    return pl.pallas_call(
        _attn_kernel,
        grid=(b, h // g, nq),
        in_specs=[
            pl.BlockSpec((4, HEAD_DIM), lambda bi, hi, qi: (0, 0)),
            pl.BlockSpec((1, g, V_DIM, t), lambda bi, hi, qi: (bi, hi, 0, qi)),
            pl.BlockSpec((1, s, g * V_DIM), lambda bi, hi, qi: (bi, 0, kblk0 + hi)),
            pl.BlockSpec((1, g, nq, V_DIM, t), lambda bi, hi, qi: (bi, hi, 0, 0, 0)),
            pl.BlockSpec((g, t, t), lambda bi, hi, qi: (hi, 0, 0)),
            pl.BlockSpec((g, t, t), lambda bi, hi, qi: (hi, 0, 0)),
            pl.BlockSpec((g, 8, t), lambda bi, hi, qi: (hi, 0, 0)),
            pl.BlockSpec((V_DIM, 1), lambda bi, hi, qi: (0, 0)),
        ],
        out_specs=pl.BlockSpec((1, g, V_DIM, t), lambda bi, hi, qi: (bi, hi, 0, qi)),
        out_shape=jax.ShapeDtypeStruct((b, h, V_DIM, s), BF16),
        scratch_shapes=[
            pltpu.VMEM((g, 2, V_DIM, t), BF16),
            pltpu.VMEM((2 * g, t, t), F32),
            pltpu.VMEM((2 * g, 8, t), F32),
            pltpu.VMEM((2 * g, 8, t), F32),
            pltpu.VMEM((2 * g, V_DIM, t), F32),
        ],
        compiler_params=pltpu.CompilerParams(
            dimension_semantics=("parallel", "parallel", "arbitrary"), vmem_limit_bytes=VMEM_LIMIT),
        name="attention",
    )(lamp, qT, zqkv, vT5, bias, biasd, slope, gsub)


def _rglru_kernel(xb_ref, gb_ref, cw_ref, cb_ref, wri_ref, br_ref, bi_ref, lam_ref,
                  o_ref, cx_sc, ch_sc, a_sc, u_sc):
    ts = xb_ref.shape[1]
    sblk = pl.program_id(1)

    @pl.when(sblk == 0)
    def _():
        cx_sc[...] = jnp.zeros(cx_sc.shape, F32)
        ch_sc[...] = jnp.zeros(ch_sc.shape, F32)

    x = xb_ref[0]
    row = lax.broadcasted_iota(jnp.int32, (ts, D_RNN), 0)
    row8 = lax.broadcasted_iota(jnp.int32, (8, D_RNN), 0)
    prev = cx_sc[...]
    cw = cw_ref[...]
    xc = cb_ref[...] + cw[CONV_WIDTH - 1:CONV_WIDTH] * x
    for d in range(1, CONV_WIDTH):
        xs = pltpu.roll(x, d, 0)
        head = jnp.where(row8 < d, pltpu.roll(prev, d, 0), xs[0:8])
        xs = jnp.concatenate([head, xs[8:]], axis=0)
        xc = xc + cw[CONV_WIDTH - 1 - d:CONV_WIDTH - d] * xs
    cx_sc[...] = x[ts - 8:ts]

    xcb = xc.astype(BF16)
    r_parts, i_parts = [], []
    for n in range(N_RNN_BLOCKS):
        ri = jnp.dot(xcb[:, n * RNN_BLOCK:(n + 1) * RNN_BLOCK], wri_ref[n], preferred_element_type=F32)
        r_parts.append(ri[:, :RNN_BLOCK])
        i_parts.append(ri[:, RNN_BLOCK:])
    r = jax.nn.sigmoid(jnp.concatenate(r_parts, axis=1) + br_ref[...])
    ig = jax.nn.sigmoid(jnp.concatenate(i_parts, axis=1) + bi_ref[...])

    nl = -lam_ref[...]
    sp = jnp.maximum(nl, 0.0) + jnp.log(1.0 + jnp.exp(-jnp.abs(nl)))
    log_a = (-LRU_C) * sp * r
    a = jnp.exp(log_a)
    mult = jnp.sqrt(1.0 - jnp.exp(2.0 * log_a))
    mult = jnp.where(jnp.logical_and(row == 0, sblk == 0), 1.0, mult)
    u = mult * (ig * xc)

    rm8 = jnp.bitwise_and(row, 7)
    for k in (1, 2, 4):
        keep = rm8 >= k
        a_sh = jnp.where(keep, pltpu.roll(a, k, 0), 1.0)
        u_sh = jnp.where(keep, pltpu.roll(u, k, 0), 0.0)
        u = a * u_sh + u
        a = a * a_sh
    a_sc[...] = a
    u_sc[...] = u

    def slab(rg, c):
        off = pl.multiple_of(rg * 8, 8)
        hblk = u_sc[pl.ds(off, 8), :] + a_sc[pl.ds(off, 8), :] * c
        u_sc[pl.ds(off, 8), :] = hblk
        return jnp.broadcast_to(hblk[7:8, :], (8, D_RNN))

    ch_sc[...] = lax.fori_loop(0, ts // 8, slab, ch_sc[...])
    o_ref[0] = (u_sc[...] * jax.nn.gelu(gb_ref[0])).astype(o_ref.dtype)


def _rglru(zrest, conv_w, conv_b, w_ri, b_r, b_i, lru_lambda):
    b, s, _ = zrest.shape
    ts = RNN_TS
    vec = lambda: pl.BlockSpec((1, D_RNN), lambda bi, si: (0, 0))
    return pl.pallas_call(
        _rglru_kernel,
        grid=(b, s // ts),
        in_specs=[
            pl.BlockSpec((1, ts, D_RNN), lambda bi, si: (bi, si, 0)),
            pl.BlockSpec((1, ts, D_RNN), lambda bi, si: (bi, si, 1)),
            pl.BlockSpec((CONV_WIDTH, D_RNN), lambda bi, si: (0, 0)),
            vec(),
            pl.BlockSpec((N_RNN_BLOCKS, RNN_BLOCK, 2 * RNN_BLOCK), lambda bi, si: (0, 0, 0)),
            vec(), vec(), vec(),
        ],
        out_specs=pl.BlockSpec((1, ts, D_RNN), lambda bi, si: (bi, si, 0)),
        out_shape=jax.ShapeDtypeStruct((b, s, D_RNN), BF16),
        scratch_shapes=[
            pltpu.VMEM((8, D_RNN), F32),
            pltpu.VMEM((8, D_RNN), F32),
            pltpu.VMEM((ts, D_RNN), F32),
            pltpu.VMEM((ts, D_RNN), F32),
        ],
        compiler_params=pltpu.CompilerParams(
            dimension_semantics=("parallel", "arbitrary"), vmem_limit_bytes=VMEM_LIMIT),
        name="rglru",
    )(zrest, zrest, conv_w, conv_b, w_ri, b_r, b_i, lru_lambda)


def _merge_ffn_kernel(x_ref, attT_ref, rnn_ref, gts_ref, bg_ref, wa_ref, wr_ref, wo_ref,
                      gm_ref, w1_ref, w2_ref, gf_ref, o_ref):
    x = x_ref[0]
    y_att = lax.dot_general(attT_ref[0], wa_ref[...], (((0,), (0,)), ((), ())),
                            preferred_element_type=F32)
    y_rnn = jnp.dot(rnn_ref[0], wr_ref[...], preferred_element_type=F32)
    gts = jax.nn.sigmoid(gts_ref[0] + bg_ref[...])
    m = gts[:, :D_MODEL] * y_att + gts[:, D_MODEL:] * y_rnn
    x1 = x + jnp.dot(m.astype(BF16), wo_ref[...], preferred_element_type=F32)
    h2 = _rms(x1, gm_ref[...]).astype(BF16)
    f = jnp.dot(h2, w1_ref[...], preferred_element_type=F32)
    f = jnp.square(jnp.maximum(f, 0.0)).astype(BF16)
    x2 = x1 + jnp.dot(f, w2_ref[...], preferred_element_type=F32)
    o_ref[0] = _rms(x2, gf_ref[...])


def _merge_ffn(x, attT, rnn, zrest, b_gate, w_att, w_rnn, w_o, g_mlp, w1, w2, g_final):
    b, s, d = x.shape
    tm = FFN_TM
    const = lambda shape: pl.BlockSpec(shape, lambda bi, si: (0,) * len(shape),
                                       pipeline_mode=pl.Buffered(1))
    return pl.pallas_call(
        _merge_ffn_kernel,
        grid=(b, s // tm),
        in_specs=[
            pl.BlockSpec((1, tm, d), lambda bi, si: (bi, si, 0)),
            pl.BlockSpec((1, D_V, tm), lambda bi, si: (bi, 0, si)),
            pl.BlockSpec((1, tm, D_RNN), lambda bi, si: (bi, si, 0)),
            pl.BlockSpec((1, tm, 2 * d), lambda bi, si: (bi, si, 1)),
            const((1, 2 * d)),
            const((D_V, d)), const((D_RNN, d)), const((d, d)),
            const((1, d)),
            const((d, D_FF)), const((D_FF, d)),
            const((1, d)),
        ],
        out_specs=pl.BlockSpec((1, tm, d), lambda bi, si: (bi, si, 0)),
        out_shape=jax.ShapeDtypeStruct((b, s, d), F32),
        compiler_params=pltpu.CompilerParams(
            dimension_semantics=("parallel", "parallel"), vmem_limit_bytes=VMEM_LIMIT),
        name="merge_ffn",
    )(x, attT, rnn, zrest, b_gate, w_att, w_rnn, w_o, g_mlp, w1, w2, g_final)


def _alibi_tables(tile):
    slopes = 2.0 ** (-8.0 * jnp.arange(1, N_HEADS + 1, dtype=F32) / N_HEADS) * LOG2E
    kr = jnp.arange(tile, dtype=F32)[:, None]
    qr = jnp.arange(tile, dtype=F32)[None, :]
    bias = slopes[:, None, None] * (kr - qr)[None]
    biasd = jnp.where((kr > qr)[None], NEG_BIG, bias)
    slope_rows = jnp.broadcast_to((slopes * tile)[:, None, None], (N_HEADS, 8, tile))
    return bias, biasd, slope_rows


def kernel(x, w_in, b_gate, g_mix, lambda_q1, lambda_k1, lambda_q2, lambda_k2, subln_g,
           conv_w, conv_b, w_r, b_r, w_i, b_i, lru_lambda, w_att_out, w_rnn_out, w_o,
           g_mlp, w_ff1, w_ff2, g_final):
    b, s, d = x.shape
    assert (d, w_in.shape[0]) == (D_MODEL, 1) and s % ATT_TILE == 0
    t = ATT_TILE
    x2d = x.reshape(b * s, d)
    w_in_b = w_in[0].astype(BF16)

    q_scale = jnp.concatenate([jnp.full((1, D_QK), HEAD_DIM ** -0.5 * LOG2E, F32),
                               jnp.ones((1, D_QKV - D_QK), F32)], axis=1)
    zqkv = _proj(x2d, g_mix, w_in_b[:, :D_QKV], q_scale, BF16).reshape(b, s, D_QKV)
    zrest = _proj(x2d, g_mix, w_in_b[:, D_QKV:], jnp.ones((1, D_REST), F32), F32).reshape(b, s, D_REST)

    qT = zqkv[:, :, :D_QK].reshape(b, s, N_HEADS, V_DIM).transpose(0, 2, 3, 1)
    vT5 = zqkv[:, :, 2 * D_QK:].reshape(b, s // t, t, N_HEADS, V_DIM).transpose(0, 3, 1, 4, 2)
    bias, biasd, slope_rows = _alibi_tables(t)
    lamp = jnp.concatenate([lambda_q1, lambda_k1, lambda_q2, lambda_k2], axis=0).astype(F32)
    attT = _attention(lamp, qT, zqkv, vT5, bias, biasd, slope_rows, subln_g.reshape(V_DIM, 1))
    attT = attT.reshape(b, D_V, s)

    w_ri = jnp.concatenate([w_r[0], w_i[0]], axis=-1).astype(BF16)
    rnn = _rglru(zrest, conv_w[0], conv_b, w_ri, b_r, b_i, lru_lambda)

    return _merge_ffn(x, attT, rnn, zrest, b_gate,
                      w_att_out[0].astype(BF16), w_rnn_out[0].astype(BF16), w_o[0].astype(BF16),
                      g_mlp, w_ff1[0].astype(BF16), w_ff2[0].astype(BF16), g_final.reshape(1, d))
```

```python
import math

import jax
import jax.numpy as jnp
import numpy as np
from jax import lax
from jax.experimental import pallas as pl
from jax.experimental.pallas import tpu as pltpu

D_MODEL = 1024
N_HEADS = 8
HEAD_DIM = 64
V_DIM = 2 * HEAD_DIM
ACC_ROWS = V_DIM + 16
D_QK = N_HEADS * 2 * HEAD_DIM
D_V = N_HEADS * V_DIM
D_RNN = D_MODEL
N_RNN_BLOCKS = 8
RNN_BLOCK = D_RNN // N_RNN_BLOCKS
CONV_WIDTH = 4
LRU_C = 8.0
D_FF = 4 * D_MODEL
D_QKV = 2 * D_QK + D_V
EPS = 1e-6
LAM_INIT = 0.8 - 0.6 * math.exp(-0.3 * 0)
LOG2E = 1.4426950408889634
NEG_BIG = -1e30

ATT_TILE = 512
ATT_HEADS_PER_STEP = 2
RNN_TS = 256
FFN_TM = 256
VMEM_LIMIT = 56 * 1024 * 1024

F32 = jnp.float32
BF16 = jnp.bfloat16


def _rms(x, g):
    return x * lax.rsqrt(jnp.mean(x * x, axis=-1, keepdims=True) + EPS) * g


def _qkv_kernel(x_ref, g_ref, wqT_ref, wk_ref, wvT_ref, qT_ref, k_ref, vT_ref):
    tm = x_ref.shape[1]
    h = _rms(x_ref[0], g_ref[...]).astype(BF16)
    nt = (((1,), (1,)), ((), ()))
    qT = lax.dot_general(wqT_ref[...], h, nt, preferred_element_type=F32)
    qT_ref[0] = (qT * (HEAD_DIM ** -0.5 * LOG2E)).astype(BF16).reshape(N_HEADS, V_DIM, tm)
    k_ref[0] = jnp.dot(h, wk_ref[...], preferred_element_type=F32).astype(BF16)
    vT = lax.dot_general(wvT_ref[...], h, nt, preferred_element_type=F32)
    vT_ref[0, :, 0] = vT.astype(BF16).reshape(N_HEADS, V_DIM, tm)


def _qkv_proj(x, g, wqT, wk, wvT):
    b, s, d = x.shape
    tm = ATT_TILE
    const = lambda shape: pl.BlockSpec(shape, lambda bi, si: (0,) * len(shape))
    return pl.pallas_call(
        _qkv_kernel,
        grid=(b, s // tm),
        in_specs=[
            pl.BlockSpec((1, tm, d), lambda bi, si: (bi, si, 0)),
            const((1, d)), const((D_QK, d)), const((d, D_QK)), const((D_V, d)),
        ],
        out_specs=[
            pl.BlockSpec((1, N_HEADS, V_DIM, tm), lambda bi, si: (bi, 0, 0, si)),
            pl.BlockSpec((1, tm, D_QK), lambda bi, si: (bi, si, 0)),
            pl.BlockSpec((1, N_HEADS, 1, V_DIM, tm), lambda bi, si: (bi, 0, si, 0, 0)),
        ],
        out_shape=[
            jax.ShapeDtypeStruct((b, N_HEADS, V_DIM, s), BF16),
            jax.ShapeDtypeStruct((b, s, D_QK), BF16),
            jax.ShapeDtypeStruct((b, N_HEADS, s // tm, V_DIM, tm), BF16),
        ],
        compiler_params=pltpu.CompilerParams(
            dimension_semantics=("parallel", "parallel"), vmem_limit_bytes=VMEM_LIMIT),
        name="qkv_proj",
    )(x, g, wqT, wk, wvT)


def _sublane_allmax(v):
    for sh in (4, 2, 1):
        v = jnp.maximum(v, pltpu.roll(v, sh, 0))
    return v


def _attn_kernel(lamp_ref, qT_ref, k_ref, vT_ref, kext_ref, qext_ref, mask_ref, slope_ref, gsub_ref,
                 o_ref, qz_sc, s_sc, m_sc, acc_sc):
    g_heads = qT_ref.shape[1]
    tile = qT_ref.shape[3]
    i = pl.program_id(2)
    chains = [(g, c) for g in range(g_heads) for c in range(2)]
    nch = len(chains)
    ones_rows = jnp.ones((ACC_ROWS - V_DIM, tile), BF16)

    row = lax.broadcasted_iota(jnp.int32, (V_DIM, tile), 0)
    for g in range(g_heads):
        q_both = qT_ref[0, g].astype(F32)
        qz_sc[g, 0, 0:V_DIM] = jnp.where(row < HEAD_DIM, q_both, 0.0).astype(BF16)
        qz_sc[g, 1, 0:V_DIM] = jnp.where(row >= HEAD_DIM, q_both, 0.0).astype(BF16)
        qz_sc[g, 0, V_DIM:2 * V_DIM] = qext_ref[g]
        qz_sc[g, 1, V_DIM:2 * V_DIM] = qext_ref[g]
    m_sc[...] = jnp.full(m_sc.shape, NEG_BIG, F32)
    acc_sc[...] = jnp.zeros(acc_sc.shape, F32)

    def scores(j, idx):
        g, c = chains[idx]
        kt = k_ref[0, pl.ds(pl.multiple_of(j * tile, tile), tile), g * V_DIM:(g + 1) * V_DIM]
        lhs = jnp.concatenate([kt, kext_ref[...]], axis=1)
        return jnp.dot(lhs, qz_sc[g, c], preferred_element_type=F32)

    def update_acc(idx, alpha, pv):
        acc = acc_sc[idx].reshape(ACC_ROWS // 8, 8, tile) * alpha[None] + pv.reshape(ACC_ROWS // 8, 8, tile)
        acc_sc[idx] = acc.reshape(ACC_ROWS, tile)

    def step(j, diagonal):
        dist = jnp.full((8, tile), i - j, jnp.int32).astype(F32)
        pending = None
        for idx, (g, c) in enumerate(chains):
            s = s_sc[idx]
            s_next = None if diagonal else scores(j + 1, idx)
            dvec = slope_ref[g] * dist
            if diagonal:
                s = s + mask_ref[...]
            t = s.reshape(tile // 8, 8, tile)
            m_old = m_sc[idx]
            m_new = jnp.maximum(m_old, _sublane_allmax(jnp.max(t, axis=0)) - dvec)
            p = jnp.exp2(t - (m_new + dvec)[None])
            alpha = jnp.exp2(m_old - m_new)
            m_sc[idx] = m_new
            lhs = jnp.concatenate([vT_ref[0, g, j], ones_rows], axis=0)
            pv = jnp.dot(lhs, p.reshape(tile, tile).astype(BF16), preferred_element_type=F32)
            if not diagonal:
                s_sc[idx] = s_next
            if pending is not None:
                update_acc(*pending)
            pending = (idx, alpha, pv)
        update_acc(*pending)

    for idx in range(nch):
        s_sc[idx] = scores(0, idx)

    def body(j, carry):
        step(j, False)
        return carry

    lax.fori_loop(0, i, body, 0)
    step(i, True)

    lp = lamp_ref[...]
    lam = (jnp.exp(jnp.sum(lp[0:1] * lp[1:2], axis=-1, keepdims=True))
           - jnp.exp(jnp.sum(lp[2:3] * lp[3:4], axis=-1, keepdims=True)) + LAM_INIT)
    for g in range(g_heads):
        a1 = acc_sc[2 * g]
        a2 = acc_sc[2 * g + 1]
        o1 = a1[0:V_DIM] / a1[V_DIM:V_DIM + 1]
        o2 = a2[0:V_DIM] / a2[V_DIM:V_DIM + 1]
        a = o1 - lam * o2
        ms = jnp.mean(a * a, axis=0, keepdims=True)
        y = a * lax.rsqrt(ms + EPS) * gsub_ref[...] * (1.0 - LAM_INIT)
        o_ref[0, g] = y.astype(o_ref.dtype)


def _attention(lamp, qT, k, vT5, kext, qext, mask, slope, gsub):
    b, h, _, s = qT.shape
    g = ATT_HEADS_PER_STEP
    t = ATT_TILE
    nq = s // t
    return pl.pallas_call(
        _attn_kernel,
        grid=(b, h // g, nq),
        in_specs=[
            pl.BlockSpec((4, HEAD_DIM), lambda bi, hi, qi: (0, 0)),
            pl.BlockSpec((1, g, V_DIM, t), lambda bi, hi, qi: (bi, hi, 0, qi)),
            pl.BlockSpec((1, s, g * V_DIM), lambda bi, hi, qi: (bi, 0, hi)),
            pl.BlockSpec((1, g, nq, V_DIM, t), lambda bi, hi, qi: (bi, hi, 0, 0, 0)),
            pl.BlockSpec((t, V_DIM), lambda bi, hi, qi: (0, 0)),
            pl.BlockSpec((g, V_DIM, t), lambda bi, hi, qi: (hi, 0, 0)),
            pl.BlockSpec((t, t), lambda bi, hi, qi: (0, 0)),
            pl.BlockSpec((g, 8, t), lambda bi, hi, qi: (hi, 0, 0)),
            pl.BlockSpec((V_DIM, 1), lambda bi, hi, qi: (0, 0)),
        ],
        out_specs=pl.BlockSpec((1, g, V_DIM, t), lambda bi, hi, qi: (bi, hi, 0, qi)),
        out_shape=jax.ShapeDtypeStruct((b, h, V_DIM, s), BF16),
        scratch_shapes=[
            pltpu.VMEM((g, 2, 2 * V_DIM, t), BF16),
            pltpu.VMEM((2 * g, t, t), F32),
            pltpu.VMEM((2 * g, 8, t), F32),
            pltpu.VMEM((2 * g, ACC_ROWS, t), F32),
        ],
        compiler_params=pltpu.CompilerParams(
            dimension_semantics=("parallel", "parallel", "arbitrary"), vmem_limit_bytes=VMEM_LIMIT),
        name="attention",
    )(lamp, qT, k, vT5, kext, qext, mask, slope, gsub)


def _rglru_kernel(x_ref, g_ref, wxg_ref, cw_ref, cb_ref, wri_ref, br_ref, bi_ref, lam_ref,
                  o_ref, cx_sc, ch_sc, a_sc, u_sc):
    ts = x_ref.shape[1]
    sblk = pl.program_id(1)

    @pl.when(sblk == 0)
    def _():
        cx_sc[...] = jnp.zeros(cx_sc.shape, F32)
        ch_sc[...] = jnp.zeros(ch_sc.shape, F32)

    h = _rms(x_ref[0], g_ref[...]).astype(BF16)
    xg = jnp.dot(h, wxg_ref[...], preferred_element_type=F32)
    x = xg[:, :D_RNN]
    gate = xg[:, D_RNN:]
    row = lax.broadcasted_iota(jnp.int32, (ts, D_RNN), 0)
    row8 = lax.broadcasted_iota(jnp.int32, (8, D_RNN), 0)
    prev = cx_sc[...]
    cw = cw_ref[...]
    xc = cb_ref[...] + cw[CONV_WIDTH - 1:CONV_WIDTH] * x
    for d in range(1, CONV_WIDTH):
        xs = pltpu.roll(x, d, 0)
        head = jnp.where(row8 < d, pltpu.roll(prev, d, 0), xs[0:8])
        xs = jnp.concatenate([head, xs[8:]], axis=0)
        xc = xc + cw[CONV_WIDTH - 1 - d:CONV_WIDTH - d] * xs
    cx_sc[...] = x[ts - 8:ts]

    xcb = xc.astype(BF16)
    r_parts, i_parts = [], []
    for n in range(N_RNN_BLOCKS):
        ri = jnp.dot(xcb[:, n * RNN_BLOCK:(n + 1) * RNN_BLOCK], wri_ref[n], preferred_element_type=F32)
        r_parts.append(ri[:, :RNN_BLOCK])
        i_parts.append(ri[:, RNN_BLOCK:])
    r = jax.nn.sigmoid(jnp.concatenate(r_parts, axis=1) + br_ref[...])
    ig = jax.nn.sigmoid(jnp.concatenate(i_parts, axis=1) + bi_ref[...])

    nl = -lam_ref[...]
    sp = jnp.maximum(nl, 0.0) + jnp.log(1.0 + jnp.exp(-jnp.abs(nl)))
    log_a = (-LRU_C) * sp * r
    a = jnp.exp(log_a)
    mult = jnp.sqrt(1.0 - jnp.exp(2.0 * log_a))
    mult = jnp.where(jnp.logical_and(row == 0, sblk == 0), 1.0, mult)
    u = mult * (ig * xc)

    rm8 = jnp.bitwise_and(row, 7)
    for k in (1, 2, 4):
        keep = rm8 >= k
        a_sh = jnp.where(keep, pltpu.roll(a, k, 0), 1.0)
        u_sh = jnp.where(keep, pltpu.roll(u, k, 0), 0.0)
        u = a * u_sh + u
        a = a * a_sh
    a_sc[...] = a
    u_sc[...] = u

    def slab(rg, c):
        off = pl.multiple_of(rg * 8, 8)
        hblk = u_sc[pl.ds(off, 8), :] + a_sc[pl.ds(off, 8), :] * c
        u_sc[pl.ds(off, 8), :] = hblk
        return jnp.broadcast_to(hblk[7:8, :], (8, D_RNN))

    ch_sc[...] = lax.fori_loop(0, ts // 8, slab, ch_sc[...])
    o_ref[0] = (u_sc[...] * jax.nn.gelu(gate)).astype(o_ref.dtype)


def _rglru(x, g_mix, w_xg, conv_w, conv_b, w_ri, b_r, b_i, lru_lambda):
    b, s, _ = x.shape
    ts = RNN_TS
    vec = lambda: pl.BlockSpec((1, D_RNN), lambda bi, si: (0, 0))
    return pl.pallas_call(
        _rglru_kernel,
        grid=(b, s // ts),
        in_specs=[
            pl.BlockSpec((1, ts, D_MODEL), lambda bi, si: (bi, si, 0)),
            vec(),
            pl.BlockSpec((D_MODEL, 2 * D_RNN), lambda bi, si: (0, 0)),
            pl.BlockSpec((CONV_WIDTH, D_RNN), lambda bi, si: (0, 0)),
            vec(),
            pl.BlockSpec((N_RNN_BLOCKS, RNN_BLOCK, 2 * RNN_BLOCK), lambda bi, si: (0, 0, 0)),
            vec(), vec(), vec(),
        ],
        out_specs=pl.BlockSpec((1, ts, D_RNN), lambda bi, si: (bi, si, 0)),
        out_shape=jax.ShapeDtypeStruct((b, s, D_RNN), BF16),
        scratch_shapes=[
            pltpu.VMEM((8, D_RNN), F32),
            pltpu.VMEM((8, D_RNN), F32),
            pltpu.VMEM((ts, D_RNN), F32),
            pltpu.VMEM((ts, D_RNN), F32),
        ],
        compiler_params=pltpu.CompilerParams(
            dimension_semantics=("parallel", "arbitrary"), vmem_limit_bytes=VMEM_LIMIT),
        name="rglru",
    )(x, g_mix, w_xg, conv_w, conv_b, w_ri, b_r, b_i, lru_lambda)


def _merge_ffn_kernel(x_ref, attT_ref, rnn_ref, gx_ref, wg_ref, bg_ref, wa_ref, wr_ref, wo_ref,
                      gm_ref, w1_ref, w2_ref, gf_ref, o_ref):
    x = x_ref[0]
    h = _rms(x, gx_ref[...]).astype(BF16)
    y_att = lax.dot_general(attT_ref[0], wa_ref[...], (((0,), (0,)), ((), ())),
                            preferred_element_type=F32)
    y_rnn = jnp.dot(rnn_ref[0], wr_ref[...], preferred_element_type=F32)
    gts = jax.nn.sigmoid(jnp.dot(h, wg_ref[...], preferred_element_type=F32) + bg_ref[...])
    m = gts[:, :D_MODEL] * y_att + gts[:, D_MODEL:] * y_rnn
    x1 = x + jnp.dot(m.astype(BF16), wo_ref[...], preferred_element_type=F32)
    h2 = _rms(x1, gm_ref[...]).astype(BF16)
    f = jnp.dot(h2, w1_ref[...], preferred_element_type=F32)
    f = jnp.square(jnp.maximum(f, 0.0)).astype(BF16)
    x2 = x1 + jnp.dot(f, w2_ref[...], preferred_element_type=F32)
    o_ref[0] = _rms(x2, gf_ref[...])


def _merge_ffn(x, attT, rnn, g_mix, w_gate, b_gate, w_att, w_rnn, w_o, g_mlp, w1, w2, g_final):
    b, s, d = x.shape
    tm = FFN_TM
    const = lambda shape: pl.BlockSpec(shape, lambda bi, si: (0,) * len(shape),
                                       pipeline_mode=pl.Buffered(1))
    return pl.pallas_call(
        _merge_ffn_kernel,
        grid=(b, s // tm),
        in_specs=[
            pl.BlockSpec((1, tm, d), lambda bi, si: (bi, si, 0)),
            pl.BlockSpec((1, D_V, tm), lambda bi, si: (bi, 0, si)),
            pl.BlockSpec((1, tm, D_RNN), lambda bi, si: (bi, si, 0)),
            const((1, d)), const((d, 2 * d)),
            const((1, 2 * d)),
            const((D_V, d)), const((D_RNN, d)), const((d, d)),
            const((1, d)),
            const((d, D_FF)), const((D_FF, d)),
            const((1, d)),
        ],
        out_specs=pl.BlockSpec((1, tm, d), lambda bi, si: (bi, si, 0)),
        out_shape=jax.ShapeDtypeStruct((b, s, d), F32),
        compiler_params=pltpu.CompilerParams(
            dimension_semantics=("parallel", "parallel"), vmem_limit_bytes=VMEM_LIMIT),
        name="merge_ffn",
    )(x, attT, rnn, g_mix, w_gate, b_gate, w_att, w_rnn, w_o, g_mlp, w1, w2, g_final)


def _split_bf16(v, pieces=3):
    out, r = [], np.asarray(v, np.float64)
    for _ in range(pieces):
        piece = r.astype(np.float32).astype(BF16)
        out.append(piece)
        r = r - piece.astype(np.float64)
    return out


def _alibi_tables(tile):
    slopes = 2.0 ** (-8.0 * np.arange(1, N_HEADS + 1, dtype=np.float64) / N_HEADS) * LOG2E
    kr = np.arange(tile, dtype=np.float64)
    qr = np.arange(tile, dtype=np.float64)
    assert tile <= 256 * 256
    kext = np.zeros((tile, V_DIM), np.float32)
    kext[:, 0:3] = np.mod(kr, 256.0)[:, None]
    kext[:, 3:6] = np.floor(kr / 256.0)[:, None]
    kext[:, 6:9] = 1.0
    qext = np.zeros((N_HEADS, V_DIM, tile), BF16)
    for n, piece in enumerate(_split_bf16(slopes)):
        qext[:, n, :] = piece[:, None]
        qext[:, 3 + n, :] = (piece.astype(np.float32) * 256.0).astype(BF16)[:, None]
    for n, piece in enumerate(_split_bf16(-slopes[:, None] * qr[None, :])):
        qext[:, 6 + n, :] = piece
    mask = np.where(kr[:, None] > qr[None, :], NEG_BIG, 0.0).astype(np.float32)
    slope_rows = np.broadcast_to((slopes * tile).astype(np.float32)[:, None, None], (N_HEADS, 8, tile))
    return jnp.asarray(kext, BF16), jnp.asarray(qext), jnp.asarray(mask), jnp.asarray(slope_rows)


def kernel(x, w_in, b_gate, g_mix, lambda_q1, lambda_k1, lambda_q2, lambda_k2, subln_g,
           conv_w, conv_b, w_r, b_r, w_i, b_i, lru_lambda, w_att_out, w_rnn_out, w_o,
           g_mlp, w_ff1, w_ff2, g_final):
    b, s, d = x.shape
    assert (d, w_in.shape[0]) == (D_MODEL, 1) and s % ATT_TILE == 0
    t = ATT_TILE
    w_in_b = w_in[0].astype(BF16)
    w_q, w_k, w_v = (w_in_b[:, n * D_QK:(n + 1) * D_QK] for n in range(3))
    qT, k, vT5 = _qkv_proj(x, g_mix, w_q.T, w_k, w_v.T)
    kext, qext, mask, slope_rows = _alibi_tables(t)
    lamp = jnp.concatenate([lambda_q1, lambda_k1, lambda_q2, lambda_k2], axis=0).astype(F32)
    attT = _attention(lamp, qT, k, vT5, kext, qext, mask, slope_rows, subln_g.reshape(V_DIM, 1))
    attT = attT.reshape(b, D_V, s)

    w_ri = jnp.concatenate([w_r[0], w_i[0]], axis=-1).astype(BF16)
    rnn = _rglru(x, g_mix, w_in_b[:, D_QKV:D_QKV + 2 * D_RNN], conv_w[0], conv_b, w_ri, b_r, b_i, lru_lambda)

    return _merge_ffn(x, attT, rnn, g_mix, w_in_b[:, D_QKV + 2 * D_RNN:], b_gate,
                      w_att_out[0].astype(BF16), w_rnn_out[0].astype(BF16), w_o[0].astype(BF16),
                      g_mlp, w_ff1[0].astype(BF16), w_ff2[0].astype(BF16), g_final.reshape(1, d))
```

```python
import math

import jax
import jax.numpy as jnp
import numpy as np
from jax import lax
from jax.experimental import pallas as pl
from jax.experimental.pallas import tpu as pltpu

D_MODEL = 1024
N_HEADS = 8
HEAD_DIM = 64
V_DIM = 2 * HEAD_DIM
ACC_ROWS = V_DIM + 16
D_QK = N_HEADS * 2 * HEAD_DIM
D_V = N_HEADS * V_DIM
D_RNN = D_MODEL
N_RNN_BLOCKS = 8
RNN_BLOCK = D_RNN // N_RNN_BLOCKS
CONV_WIDTH = 4
LRU_C = 8.0
D_FF = 4 * D_MODEL
D_QKV = 2 * D_QK + D_V
EPS = 1e-6
LAM_INIT = 0.8 - 0.6 * math.exp(-0.3 * 0)
LOG2E = 1.4426950408889634
NEG_BIG = -1e30
TINY = 1e-30

ATT_TILE = 512
ATT_HEADS_PER_STEP = 4
RNN_TS = 256
FFN_TM = 256
VMEM_LIMIT = 56 * 1024 * 1024

F32 = jnp.float32
BF16 = jnp.bfloat16


def _rms(x, g):
    return x * lax.rsqrt(jnp.mean(x * x, axis=-1, keepdims=True) + EPS) * g


def _qkv_kernel(x_ref, g_ref, wqT_ref, wk_ref, wvT_ref, qT_ref, k_ref, vT_ref):
    tm = x_ref.shape[1]
    h = _rms(x_ref[0], g_ref[...]).astype(BF16)
    nt = (((1,), (1,)), ((), ()))
    qT = lax.dot_general(wqT_ref[...], h, nt, preferred_element_type=F32)
    qT_ref[0] = (qT * (HEAD_DIM ** -0.5 * LOG2E)).astype(BF16).reshape(N_HEADS, V_DIM, tm)
    k_ref[0] = jnp.dot(h, wk_ref[...], preferred_element_type=F32).astype(BF16)
    vT = lax.dot_general(wvT_ref[...], h, nt, preferred_element_type=F32)
    vT_ref[0, :, 0] = vT.astype(BF16).reshape(N_HEADS, V_DIM, tm)


def _qkv_proj(x, g, wqT, wk, wvT):
    b, s, d = x.shape
    tm = ATT_TILE
    const = lambda shape: pl.BlockSpec(shape, lambda bi, si: (0,) * len(shape))
    return pl.pallas_call(
        _qkv_kernel,
        grid=(b, s // tm),
        in_specs=[
            pl.BlockSpec((1, tm, d), lambda bi, si: (bi, si, 0)),
            const((1, d)), const((D_QK, d)), const((d, D_QK)), const((D_V, d)),
        ],
        out_specs=[
            pl.BlockSpec((1, N_HEADS, V_DIM, tm), lambda bi, si: (bi, 0, 0, si)),
            pl.BlockSpec((1, tm, D_QK), lambda bi, si: (bi, si, 0)),
            pl.BlockSpec((1, N_HEADS, 1, V_DIM, tm), lambda bi, si: (bi, 0, si, 0, 0)),
        ],
        out_shape=[
            jax.ShapeDtypeStruct((b, N_HEADS, V_DIM, s), BF16),
            jax.ShapeDtypeStruct((b, s, D_QK), BF16),
            jax.ShapeDtypeStruct((b, N_HEADS, s // tm, V_DIM, tm), BF16),
        ],
        compiler_params=pltpu.CompilerParams(
            dimension_semantics=("parallel", "parallel"), vmem_limit_bytes=VMEM_LIMIT),
        name="qkv_proj",
    )(x, g, wqT, wk, wvT)


def _sublane_allmax(v):
    for sh in (4, 2, 1):
        v = jnp.maximum(v, pltpu.roll(v, sh, 0))
    return v


def _attn_kernel(lamp_ref, qT_ref, qTn_ref, k_ref, vT_ref, kext_ref, qext_ref, mask_ref, slope_ref, gsub_ref,
                 o_ref, qz_sc, s_sc, m_sc, acc_sc):
    g_heads = qT_ref.shape[1]
    tile = qT_ref.shape[3]
    i = pl.program_id(2)
    chains = [(g, c) for g in range(g_heads) for c in range(2)]
    nch = len(chains)
    ones_rows = jnp.ones((ACC_ROWS - V_DIM, tile), BF16)
    cur = jnp.bitwise_and(i, 1)

    def build_qz(src_ref, slot):
        row = lax.broadcasted_iota(jnp.int32, (V_DIM, tile), 0)
        for g in range(g_heads):
            q_both = src_ref[0, g].astype(F32)
            qz_sc[slot, g, 0, 0:V_DIM] = jnp.where(row < HEAD_DIM, q_both, 0.0).astype(BF16)
            qz_sc[slot, g, 1, 0:V_DIM] = jnp.where(row >= HEAD_DIM, q_both, 0.0).astype(BF16)
            qz_sc[slot, g, 0, V_DIM:2 * V_DIM] = qext_ref[g]
            qz_sc[slot, g, 1, V_DIM:2 * V_DIM] = qext_ref[g]

    def scores(slot, j, idx):
        g, c = chains[idx]
        kt = k_ref[0, pl.ds(pl.multiple_of(j * tile, tile), tile), g * V_DIM:(g + 1) * V_DIM]
        lhs = jnp.concatenate([kt, kext_ref[...]], axis=1)
        return jnp.dot(lhs, qz_sc[slot, g, c], preferred_element_type=F32)

    @pl.when(i == 0)
    def _():
        build_qz(qT_ref, 0)
        for idx in range(nch):
            s_sc[idx] = scores(0, 0, idx)

    m_sc[...] = jnp.full(m_sc.shape, NEG_BIG, F32)
    acc_sc[...] = jnp.zeros(acc_sc.shape, F32)

    def update_acc(idx, alpha, pv):
        acc = acc_sc[idx].reshape(ACC_ROWS // 8, 8, tile) * alpha[None] + pv.reshape(ACC_ROWS // 8, 8, tile)
        acc_sc[idx] = acc.reshape(ACC_ROWS, tile)

    def step(j, diagonal):
        dist = jnp.full((8, tile), i - j, jnp.int32).astype(F32)
        pending = None
        for idx, (g, c) in enumerate(chains):
            s = s_sc[idx]
            s_next = scores(1 - cur, 0, idx) if diagonal else scores(cur, j + 1, idx)
            dvec = slope_ref[g] * dist
            if diagonal:
                s = s + mask_ref[...]
            t = s.reshape(tile // 8, 8, tile)
            m_old = m_sc[idx]
            m_new = jnp.maximum(m_old, _sublane_allmax(jnp.max(t, axis=0)) - dvec)
            p = jnp.exp2(t - (m_new + dvec)[None])
            alpha = jnp.exp2(m_old - m_new)
            m_sc[idx] = m_new
            lhs = jnp.concatenate([vT_ref[0, g, j], ones_rows], axis=0)
            pv = jnp.dot(lhs, p.reshape(tile, tile).astype(BF16), preferred_element_type=F32)
            s_sc[idx] = s_next
            if pending is not None:
                update_acc(*pending)
            pending = (idx, alpha, pv)
        update_acc(*pending)

    def body(j, carry):
        step(j, False)
        return carry

    lax.fori_loop(0, i, body, 0)
    build_qz(qTn_ref, 1 - cur)
    step(i, True)

    lp = lamp_ref[...]
    lam = (jnp.exp(jnp.sum(lp[0:1] * lp[1:2], axis=-1, keepdims=True))
           - jnp.exp(jnp.sum(lp[2:3] * lp[3:4], axis=-1, keepdims=True)) + LAM_INIT)
    for g in range(g_heads):
        a1 = acc_sc[2 * g]
        a2 = acc_sc[2 * g + 1]
        o1 = a1[0:V_DIM] / a1[V_DIM:V_DIM + 1]
        o2 = a2[0:V_DIM] / a2[V_DIM:V_DIM + 1]
        a = o1 - lam * o2
        ms = jnp.mean(a * a, axis=0, keepdims=True)
        y = a * lax.rsqrt(ms + EPS) * gsub_ref[...] * (1.0 - LAM_INIT)
        o_ref[0, g] = y.astype(o_ref.dtype)


def _attention(lamp, qT, k, vT5, kext, qext, mask, slope, gsub):
    b, h, _, s = qT.shape
    g = ATT_HEADS_PER_STEP
    t = ATT_TILE
    nq = s // t
    return pl.pallas_call(
        _attn_kernel,
        grid=(b, h // g, nq),
        in_specs=[
            pl.BlockSpec((4, HEAD_DIM), lambda bi, hi, qi: (0, 0)),
            pl.BlockSpec((1, g, V_DIM, t), lambda bi, hi, qi: (bi, hi, 0, qi)),
            pl.BlockSpec((1, g, V_DIM, t), lambda bi, hi, qi: (bi, hi, 0, jnp.minimum(qi + 1, nq - 1))),
            pl.BlockSpec((1, s, g * V_DIM), lambda bi, hi, qi: (bi, 0, hi)),
            pl.BlockSpec((1, g, nq, V_DIM, t), lambda bi, hi, qi: (bi, hi, 0, 0, 0)),
            pl.BlockSpec((t, V_DIM), lambda bi, hi, qi: (0, 0)),
            pl.BlockSpec((g, V_DIM, t), lambda bi, hi, qi: (hi, 0, 0)),
            pl.BlockSpec((t, t), lambda bi, hi, qi: (0, 0)),
            pl.BlockSpec((g, 8, t), lambda bi, hi, qi: (hi, 0, 0)),
            pl.BlockSpec((V_DIM, 1), lambda bi, hi, qi: (0, 0)),
        ],
        out_specs=pl.BlockSpec((1, g, V_DIM, t), lambda bi, hi, qi: (bi, hi, 0, qi)),
        out_shape=jax.ShapeDtypeStruct((b, h, V_DIM, s), BF16),
        scratch_shapes=[
            pltpu.VMEM((2, g, 2, 2 * V_DIM, t), BF16),
            pltpu.VMEM((2 * g, t, t), F32),
            pltpu.VMEM((2 * g, 8, t), F32),
            pltpu.VMEM((2 * g, ACC_ROWS, t), F32),
        ],
        compiler_params=pltpu.CompilerParams(
            dimension_semantics=("parallel", "parallel", "arbitrary"), vmem_limit_bytes=VMEM_LIMIT),
        name="attention",
    )(lamp, qT, qT, k, vT5, kext, qext, mask, slope, gsub)


def _rglru_kernel(x_ref, g_ref, wxg_ref, cw_ref, cb_ref, wri_ref, br_ref, bi_ref, lam_ref,
                  o_ref, xg_sc, o_sc, ch_sc, al_sc, ul_sc, c_sc):
    ts = x_ref.shape[1]
    nr = ts // 8
    nb = D_RNN // RNN_BLOCK
    sblk = pl.program_id(1)

    @pl.when(sblk == 0)
    def _():
        xg_sc[0:nb, 0:8, :] = jnp.zeros((nb, 8, RNN_BLOCK), F32)
        ch_sc[...] = jnp.zeros(ch_sc.shape, F32)

    h = _rms(x_ref[0], g_ref[...]).astype(BF16)
    xg = jnp.dot(h, wxg_ref[...], preferred_element_type=F32)
    for c in range(2 * nb):
        xg_sc[c, 8:8 + ts, :] = xg[:, c * RNN_BLOCK:(c + 1) * RNN_BLOCK]

    def phase(col0, s):
        return jnp.concatenate([xg_sc[col0 + c, pl.ds(8 + s, nr, stride=8), :] for c in range(nb)], axis=1)

    xph = {s: phase(0, s) for s in range(1 - CONV_WIDTH, 8)}
    cw = cw_ref[...]
    xc = jnp.concatenate(
        [cb_ref[...] + sum(cw[CONV_WIDTH - 1 - d:CONV_WIDTH - d] * xph[s - d] for d in range(CONV_WIDTH))
         for s in range(8)], axis=0)
    for c in range(nb):
        xg_sc[c, 0:8, :] = xg_sc[c, ts:ts + 8, :]

    xcb = xc.astype(BF16)
    r_parts, i_parts = [], []
    for n in range(N_RNN_BLOCKS):
        ri = jnp.dot(xcb[:, n * RNN_BLOCK:(n + 1) * RNN_BLOCK], wri_ref[n], preferred_element_type=F32)
        r_parts.append(ri[:, :RNN_BLOCK])
        i_parts.append(ri[:, RNN_BLOCK:])
    r = jax.nn.sigmoid(jnp.concatenate(r_parts, axis=1) + br_ref[...])
    ig = jax.nn.sigmoid(jnp.concatenate(i_parts, axis=1) + bi_ref[...])

    nl = -lam_ref[...]
    sp = jnp.maximum(nl, 0.0) + jnp.log(1.0 + jnp.exp(-jnp.abs(nl)))
    a = jnp.exp2((-LRU_C * LOG2E) * sp * r)
    y = 1.0 - a * a
    mult = y * lax.rsqrt(jnp.maximum(y, TINY))
    row = lax.broadcasted_iota(jnp.int32, (ts, D_RNN), 0)
    mult = jnp.where(jnp.logical_and(row == 0, sblk == 0), 1.0, mult)
    u = mult * (ig * xc)

    hloc, ploc = [u[0:nr]], [a[0:nr]]
    for s in range(1, 8):
        a_s = a[s * nr:(s + 1) * nr]
        hloc.append(a_s * hloc[-1] + u[s * nr:(s + 1) * nr])
        ploc.append(a_s * ploc[-1])
    ul_sc[...] = hloc[7]
    al_sc[...] = ploc[7]
    c = ch_sc[...]
    for rr in range(nr):
        c_sc[rr:rr + 1, :] = c
        c = ul_sc[rr:rr + 1, :] + al_sc[rr:rr + 1, :] * c
    ch_sc[...] = c
    c_in = c_sc[...]

    for s in range(8):
        out = (hloc[s] + ploc[s] * c_in) * jax.nn.gelu(phase(nb, s))
        for c in range(nb):
            o_sc[c, pl.ds(s, nr, stride=8), :] = out[:, c * RNN_BLOCK:(c + 1) * RNN_BLOCK]
    o_ref[0] = jnp.concatenate([o_sc[c] for c in range(nb)], axis=1).astype(o_ref.dtype)


def _rglru(x, g_mix, w_xg, conv_w, conv_b, w_ri, b_r, b_i, lru_lambda):
    b, s, _ = x.shape
    ts = RNN_TS
    nb = D_RNN // RNN_BLOCK
    vec = lambda: pl.BlockSpec((1, D_RNN), lambda bi, si: (0, 0))
    return pl.pallas_call(
        _rglru_kernel,
        grid=(b, s // ts),
        in_specs=[
            pl.BlockSpec((1, ts, D_MODEL), lambda bi, si: (bi, si, 0)),
            vec(),
            pl.BlockSpec((D_MODEL, 2 * D_RNN), lambda bi, si: (0, 0)),
            pl.BlockSpec((CONV_WIDTH, D_RNN), lambda bi, si: (0, 0)),
            vec(),
            pl.BlockSpec((N_RNN_BLOCKS, RNN_BLOCK, 2 * RNN_BLOCK), lambda bi, si: (0, 0, 0)),
            vec(), vec(), vec(),
        ],
        out_specs=pl.BlockSpec((1, ts, D_RNN), lambda bi, si: (bi, si, 0)),
        out_shape=jax.ShapeDtypeStruct((b, s, D_RNN), BF16),
        scratch_shapes=[
            pltpu.VMEM((2 * nb, ts + 8, RNN_BLOCK), F32),
            pltpu.VMEM((nb, ts, RNN_BLOCK), F32),
            pltpu.VMEM((1, D_RNN), F32),
            pltpu.VMEM((ts // 8, D_RNN), F32),
            pltpu.VMEM((ts // 8, D_RNN), F32),
            pltpu.VMEM((ts // 8, D_RNN), F32),
        ],
        compiler_params=pltpu.CompilerParams(
            dimension_semantics=("parallel", "arbitrary"), vmem_limit_bytes=VMEM_LIMIT),
        name="rglru",
    )(x, g_mix, w_xg, conv_w, conv_b, w_ri, b_r, b_i, lru_lambda)


def _merge_ffn_kernel(x_ref, attT_ref, rnn_ref, gx_ref, wg_ref, bg_ref, wa_ref, wr_ref, wo_ref,
                      gm_ref, w1_ref, w2_ref, gf_ref, o_ref):
    x = x_ref[0]
    h = _rms(x, gx_ref[...]).astype(BF16)
    y_att = lax.dot_general(attT_ref[0], wa_ref[...], (((0,), (0,)), ((), ())),
                            preferred_element_type=F32)
    y_rnn = jnp.dot(rnn_ref[0], wr_ref[...], preferred_element_type=F32)
    gts = jax.nn.sigmoid(jnp.dot(h, wg_ref[...], preferred_element_type=F32) + bg_ref[...])
    m = gts[:, :D_MODEL] * y_att + gts[:, D_MODEL:] * y_rnn
    x1 = x + jnp.dot(m.astype(BF16), wo_ref[...], preferred_element_type=F32)
    h2 = _rms(x1, gm_ref[...]).astype(BF16)
    f = jnp.dot(h2, w1_ref[...], preferred_element_type=F32)
    f = jnp.square(jnp.maximum(f, 0.0)).astype(BF16)
    x2 = x1 + jnp.dot(f, w2_ref[...], preferred_element_type=F32)
    o_ref[0] = _rms(x2, gf_ref[...])


def _merge_ffn(x, attT, rnn, g_mix, w_gate, b_gate, w_att, w_rnn, w_o, g_mlp, w1, w2, g_final):
    b, s, d = x.shape
    tm = FFN_TM
    const = lambda shape: pl.BlockSpec(shape, lambda bi, si: (0,) * len(shape),
                                       pipeline_mode=pl.Buffered(1))
    return pl.pallas_call(
        _merge_ffn_kernel,
        grid=(b, s // tm),
        in_specs=[
            pl.BlockSpec((1, tm, d), lambda bi, si: (bi, si, 0)),
            pl.BlockSpec((1, D_V, tm), lambda bi, si: (bi, 0, si)),
            pl.BlockSpec((1, tm, D_RNN), lambda bi, si: (bi, si, 0)),
            const((1, d)), const((d, 2 * d)),
            const((1, 2 * d)),
            const((D_V, d)), const((D_RNN, d)), const((d, d)),
            const((1, d)),
            const((d, D_FF)), const((D_FF, d)),
            const((1, d)),
        ],
        out_specs=pl.BlockSpec((1, tm, d), lambda bi, si: (bi, si, 0)),
        out_shape=jax.ShapeDtypeStruct((b, s, d), F32),
        compiler_params=pltpu.CompilerParams(
            dimension_semantics=("parallel", "parallel"), vmem_limit_bytes=VMEM_LIMIT),
        name="merge_ffn",
    )(x, attT, rnn, g_mix, w_gate, b_gate, w_att, w_rnn, w_o, g_mlp, w1, w2, g_final)


def _split_bf16(v, pieces=3):
    out, r = [], np.asarray(v, np.float64)
    for _ in range(pieces):
        piece = r.astype(np.float32).astype(BF16)
        out.append(piece)
        r = r - piece.astype(np.float64)
    return out


def _alibi_tables(tile):
    slopes = 2.0 ** (-8.0 * np.arange(1, N_HEADS + 1, dtype=np.float64) / N_HEADS) * LOG2E
    kr = np.arange(tile, dtype=np.float64)
    qr = np.arange(tile, dtype=np.float64)
    assert tile <= 256 * 256
    kext = np.zeros((tile, V_DIM), np.float32)
    kext[:, 0:3] = np.mod(kr, 256.0)[:, None]
    kext[:, 3:6] = np.floor(kr / 256.0)[:, None]
    kext[:, 6:9] = 1.0
    qext = np.zeros((N_HEADS, V_DIM, tile), BF16)
    for n, piece in enumerate(_split_bf16(slopes)):
        qext[:, n, :] = piece[:, None]
        qext[:, 3 + n, :] = (piece.astype(np.float32) * 256.0).astype(BF16)[:, None]
    for n, piece in enumerate(_split_bf16(-slopes[:, None] * qr[None, :])):
        qext[:, 6 + n, :] = piece
    mask = np.where(kr[:, None] > qr[None, :], NEG_BIG, 0.0).astype(np.float32)
    slope_rows = np.broadcast_to((slopes * tile).astype(np.float32)[:, None, None], (N_HEADS, 8, tile))
    return jnp.asarray(kext, BF16), jnp.asarray(qext), jnp.asarray(mask), jnp.asarray(slope_rows)


def kernel(x, w_in, b_gate, g_mix, lambda_q1, lambda_k1, lambda_q2, lambda_k2, subln_g,
           conv_w, conv_b, w_r, b_r, w_i, b_i, lru_lambda, w_att_out, w_rnn_out, w_o,
           g_mlp, w_ff1, w_ff2, g_final):
    b, s, d = x.shape
    assert (d, w_in.shape[0]) == (D_MODEL, 1) and s % ATT_TILE == 0
    t = ATT_TILE
    w_in_b = w_in[0].astype(BF16)
    w_q, w_k, w_v = (w_in_b[:, n * D_QK:(n + 1) * D_QK] for n in range(3))
    qT, k, vT5 = _qkv_proj(x, g_mix, w_q.T, w_k, w_v.T)
    kext, qext, mask, slope_rows = _alibi_tables(t)
    lamp = jnp.concatenate([lambda_q1, lambda_k1, lambda_q2, lambda_k2], axis=0).astype(F32)
    attT = _attention(lamp, qT, k, vT5, kext, qext, mask, slope_rows, subln_g.reshape(V_DIM, 1))
    attT = attT.reshape(b, D_V, s)

    w_ri = jnp.concatenate([w_r[0], w_i[0]], axis=-1).astype(BF16)
    rnn = _rglru(x, g_mix, w_in_b[:, D_QKV:D_QKV + 2 * D_RNN], conv_w[0], conv_b, w_ri, b_r, b_i, lru_lambda)

    return _merge_ffn(x, attT, rnn, g_mix, w_in_b[:, D_QKV + 2 * D_RNN:], b_gate,
                      w_att_out[0].astype(BF16), w_rnn_out[0].astype(BF16), w_o[0].astype(BF16),
                      g_mlp, w_ff1[0].astype(BF16), w_ff2[0].astype(BF16), g_final.reshape(1, d))
```

```python
import math

import jax
import jax.numpy as jnp
import numpy as np
from jax import lax
from jax.experimental import pallas as pl
from jax.experimental.pallas import tpu as pltpu

D_MODEL = 1024
N_HEADS = 8
HEAD_DIM = 64
V_DIM = 2 * HEAD_DIM
ACC_ROWS = V_DIM + 16
D_QK = N_HEADS * 2 * HEAD_DIM
D_V = N_HEADS * V_DIM
D_RNN = D_MODEL
N_RNN_BLOCKS = 8
RNN_BLOCK = D_RNN // N_RNN_BLOCKS
CONV_WIDTH = 4
LRU_C = 8.0
D_FF = 4 * D_MODEL
D_QKV = 2 * D_QK + D_V
EPS = 1e-6
LAM_INIT = 0.8 - 0.6 * math.exp(-0.3 * 0)
LOG2E = 1.4426950408889634
NEG_BIG = -1e30
TINY = 1e-30

ATT_TILE = 512
ATT_HEADS_PER_STEP = 4
FFN_TM = 256
VMEM_LIMIT = 56 * 1024 * 1024

F32 = jnp.float32
BF16 = jnp.bfloat16


def _rms(x, g):
    return x * lax.rsqrt(jnp.mean(x * x, axis=-1, keepdims=True) + EPS) * g


def _zero_row(v):
    zero = jnp.minimum(jnp.abs(v[0:1, 0:RNN_BLOCK]), 0.0)
    return jnp.concatenate([zero] * (D_RNN // RNN_BLOCK), axis=1)


def _qkv_rglru_kernel(x_ref, g_ref, wqT_ref, wk_ref, wvT_ref, wxg_ref, cw_ref, cb_ref, wri_ref, br_ref, bi_ref,
                      lam_ref, qT_ref, k_ref, vT_ref, o_ref, xg_sc, o_sc, ch_sc, al_sc, ul_sc, c_sc):
    ts = x_ref.shape[1]
    nr = ts // 8
    nb = D_RNN // RNN_BLOCK
    sblk = pl.program_id(1)
    nt = (((1,), (1,)), ((), ()))

    @pl.when(sblk == 0)
    def _():
        xg_sc[0:nb, 0:8, :] = jnp.zeros((nb, 8, RNN_BLOCK), F32)
        ch_sc[...] = jnp.zeros(ch_sc.shape, F32)

    h = _rms(x_ref[0], g_ref[...]).astype(BF16)
    xg = jnp.dot(h, wxg_ref[...], preferred_element_type=F32)
    for c in range(2 * nb):
        xg_sc[c, 8:8 + ts, :] = xg[:, c * RNN_BLOCK:(c + 1) * RNN_BLOCK]

    def phase(col0, s):
        return jnp.concatenate([xg_sc[col0 + c, pl.ds(8 + s, nr, stride=8), :] for c in range(nb)], axis=1)

    xph = {s: phase(0, s) for s in range(1 - CONV_WIDTH, 8)}
    cw = cw_ref[...]
    xc = jnp.concatenate(
        [cb_ref[...] + sum(cw[CONV_WIDTH - 1 - d:CONV_WIDTH - d] * xph[s - d] for d in range(CONV_WIDTH))
         for s in range(8)], axis=0)
    for c in range(nb):
        xg_sc[c, 0:8, :] = xg_sc[c, ts:ts + 8, :]

    xcb = xc.astype(BF16)
    r_parts, i_parts = [], []
    for n in range(N_RNN_BLOCKS):
        ri = jnp.dot(xcb[:, n * RNN_BLOCK:(n + 1) * RNN_BLOCK], wri_ref[n], preferred_element_type=F32)
        r_parts.append(ri[:, :RNN_BLOCK])
        i_parts.append(ri[:, RNN_BLOCK:])

    qT = lax.dot_general(wqT_ref[...], h, nt, preferred_element_type=F32)
    qT_ref[0] = (qT * (HEAD_DIM ** -0.5 * LOG2E)).astype(BF16).reshape(N_HEADS, V_DIM, ts)

    r = jax.nn.sigmoid(jnp.concatenate(r_parts, axis=1) + br_ref[...])
    ig = jax.nn.sigmoid(jnp.concatenate(i_parts, axis=1) + bi_ref[...])
    nl = -lam_ref[...]
    sp = jnp.maximum(nl, 0.0) + jnp.log(1.0 + jnp.exp(-jnp.abs(nl)))
    a = jnp.exp2((-LRU_C * LOG2E) * sp * r)
    y = 1.0 - a * a
    mult = y * lax.rsqrt(jnp.maximum(y, TINY))
    row = lax.broadcasted_iota(jnp.int32, (ts, D_RNN), 0)
    mult = jnp.where(jnp.logical_and(row == 0, sblk == 0), 1.0, mult)
    u = mult * (ig * xc)

    kk = jnp.dot(h, wk_ref[...], preferred_element_type=F32)
    k_ref[0] = kk.astype(BF16)

    hloc, ploc = [u[0:nr] + _zero_row(qT)], [a[0:nr]]
    for s in range(1, 8):
        a_s = a[s * nr:(s + 1) * nr]
        hloc.append(a_s * hloc[-1] + u[s * nr:(s + 1) * nr])
        ploc.append(a_s * ploc[-1])

    vT = lax.dot_general(wvT_ref[...], h, nt, preferred_element_type=F32)
    vT_ref[0, :, 0] = vT.astype(BF16).reshape(N_HEADS, V_DIM, ts)

    ul_sc[...] = hloc[7]
    al_sc[...] = ploc[7]
    c = ch_sc[...] + _zero_row(kk)
    for rr in range(nr):
        c_sc[rr:rr + 1, :] = c
        c = ul_sc[rr:rr + 1, :] + al_sc[rr:rr + 1, :] * c
    ch_sc[...] = c
    c_in = c_sc[...]

    for s in range(8):
        out = (hloc[s] + ploc[s] * c_in) * jax.nn.gelu(phase(nb, s))
        if s == 7:
            out = out + _zero_row(vT)
        for c in range(nb):
            o_sc[c, pl.ds(s, nr, stride=8), :] = out[:, c * RNN_BLOCK:(c + 1) * RNN_BLOCK]
    o_ref[0] = jnp.concatenate([o_sc[c] for c in range(nb)], axis=1).astype(o_ref.dtype)


def _qkv_rglru(x, g, wqT, wk, wvT, w_xg, conv_w, conv_b, w_ri, b_r, b_i, lru_lambda):
    b, s, d = x.shape
    ts = ATT_TILE
    nb = D_RNN // RNN_BLOCK
    const = lambda shape: pl.BlockSpec(shape, lambda bi, si: (0,) * len(shape))
    return pl.pallas_call(
        _qkv_rglru_kernel,
        grid=(b, s // ts),
        in_specs=[
            pl.BlockSpec((1, ts, d), lambda bi, si: (bi, si, 0)),
            const((1, d)), const((D_QK, d)), const((d, D_QK)), const((D_V, d)),
            const((d, 2 * D_RNN)), const((CONV_WIDTH, D_RNN)), const((1, D_RNN)),
            const((N_RNN_BLOCKS, RNN_BLOCK, 2 * RNN_BLOCK)),
            const((1, D_RNN)), const((1, D_RNN)), const((1, D_RNN)),
        ],
        out_specs=[
            pl.BlockSpec((1, N_HEADS, V_DIM, ts), lambda bi, si: (bi, 0, 0, si)),
            pl.BlockSpec((1, ts, D_QK), lambda bi, si: (bi, si, 0)),
            pl.BlockSpec((1, N_HEADS, 1, V_DIM, ts), lambda bi, si: (bi, 0, si, 0, 0)),
            pl.BlockSpec((1, ts, D_RNN), lambda bi, si: (bi, si, 0)),
        ],
        out_shape=[
            jax.ShapeDtypeStruct((b, N_HEADS, V_DIM, s), BF16),
            jax.ShapeDtypeStruct((b, s, D_QK), BF16),
            jax.ShapeDtypeStruct((b, N_HEADS, s // ts, V_DIM, ts), BF16),
            jax.ShapeDtypeStruct((b, s, D_RNN), BF16),
        ],
        scratch_shapes=[
            pltpu.VMEM((2 * nb, ts + 8, RNN_BLOCK), F32),
            pltpu.VMEM((nb, ts, RNN_BLOCK), F32),
            pltpu.VMEM((1, D_RNN), F32),
            pltpu.VMEM((ts // 8, D_RNN), F32),
            pltpu.VMEM((ts // 8, D_RNN), F32),
            pltpu.VMEM((ts // 8, D_RNN), F32),
        ],
        compiler_params=pltpu.CompilerParams(
            dimension_semantics=("parallel", "arbitrary"), vmem_limit_bytes=VMEM_LIMIT),
        name="qkv_rglru",
    )(x, g, wqT, wk, wvT, w_xg, conv_w, conv_b, w_ri, b_r, b_i, lru_lambda)


def _sublane_allmax(v):
    for sh in (4, 2, 1):
        v = jnp.maximum(v, pltpu.roll(v, sh, 0))
    return v


def _attn_kernel(lamp_ref, qT_ref, qTn_ref, k_ref, vT_ref, kext_ref, qext_ref, mask_ref, slope_ref, gsub_ref,
                 o_ref, qz_sc, s_sc, m_sc, acc_sc):
    g_heads = qT_ref.shape[1]
    tile = qT_ref.shape[3]
    i = pl.program_id(2)
    chains = [(g, c) for g in range(g_heads) for c in range(2)]
    nch = len(chains)
    ones_rows = jnp.ones((ACC_ROWS - V_DIM, tile), BF16)
    cur = jnp.bitwise_and(i, 1)

    def build_qz(src_ref, slot):
        row = lax.broadcasted_iota(jnp.int32, (V_DIM, tile), 0)
        for g in range(g_heads):
            q_both = src_ref[0, g].astype(F32)
            qz_sc[slot, g, 0, 0:V_DIM] = jnp.where(row < HEAD_DIM, q_both, 0.0).astype(BF16)
            qz_sc[slot, g, 1, 0:V_DIM] = jnp.where(row >= HEAD_DIM, q_both, 0.0).astype(BF16)
            qz_sc[slot, g, 0, V_DIM:2 * V_DIM] = qext_ref[g]
            qz_sc[slot, g, 1, V_DIM:2 * V_DIM] = qext_ref[g]

    def scores(slot, j, idx):
        g, c = chains[idx]
        kt = k_ref[0, pl.ds(pl.multiple_of(j * tile, tile), tile), g * V_DIM:(g + 1) * V_DIM]
        lhs = jnp.concatenate([kt, kext_ref[...]], axis=1)
        return jnp.dot(lhs, qz_sc[slot, g, c], preferred_element_type=F32)

    @pl.when(i == 0)
    def _():
        build_qz(qT_ref, 0)
        for idx in range(nch):
            s_sc[idx] = scores(0, 0, idx)

    m_sc[...] = jnp.full(m_sc.shape, NEG_BIG, F32)
    acc_sc[...] = jnp.zeros(acc_sc.shape, F32)

    def update_acc(idx, alpha, pv):
        acc = acc_sc[idx].reshape(ACC_ROWS // 8, 8, tile) * alpha[None] + pv.reshape(ACC_ROWS // 8, 8, tile)
        acc_sc[idx] = acc.reshape(ACC_ROWS, tile)

    def step(j, diagonal):
        dist = jnp.full((8, tile), i - j, jnp.int32).astype(F32)
        pending = None
        for idx, (g, c) in enumerate(chains):
            s = s_sc[idx]
            s_next = scores(1 - cur, 0, idx) if diagonal else scores(cur, j + 1, idx)
            dvec = slope_ref[g] * dist
            if diagonal:
                s = s + mask_ref[...]
            t = s.reshape(tile // 8, 8, tile)
            m_old = m_sc[idx]
            m_new = jnp.maximum(m_old, _sublane_allmax(jnp.max(t, axis=0)) - dvec)
            p = jnp.exp2(t - (m_new + dvec)[None])
            alpha = jnp.exp2(m_old - m_new)
            m_sc[idx] = m_new
            lhs = jnp.concatenate([vT_ref[0, g, j], ones_rows], axis=0)
            pv = jnp.dot(lhs, p.reshape(tile, tile).astype(BF16), preferred_element_type=F32)
            s_sc[idx] = s_next
            if pending is not None:
                update_acc(*pending)
            pending = (idx, alpha, pv)
        update_acc(*pending)

    def body(j, carry):
        step(j, False)
        return carry

    lax.fori_loop(0, i, body, 0)
    build_qz(qTn_ref, 1 - cur)
    step(i, True)

    lp = lamp_ref[...]
    lam = (jnp.exp(jnp.sum(lp[0:1] * lp[1:2], axis=-1, keepdims=True))
           - jnp.exp(jnp.sum(lp[2:3] * lp[3:4], axis=-1, keepdims=True)) + LAM_INIT)
    for g in range(g_heads):
        a1 = acc_sc[2 * g]
        a2 = acc_sc[2 * g + 1]
        o1 = a1[0:V_DIM] / a1[V_DIM:V_DIM + 1]
        o2 = a2[0:V_DIM] / a2[V_DIM:V_DIM + 1]
        a = o1 - lam * o2
        ms = jnp.mean(a * a, axis=0, keepdims=True)
        y = a * lax.rsqrt(ms + EPS) * gsub_ref[...] * (1.0 - LAM_INIT)
        o_ref[0, g] = y.astype(o_ref.dtype)


def _attention(lamp, qT, k, vT5, kext, qext, mask, slope, gsub):
    b, h, _, s = qT.shape
    g = ATT_HEADS_PER_STEP
    t = ATT_TILE
    nq = s // t
    return pl.pallas_call(
        _attn_kernel,
        grid=(b, h // g, nq),
        in_specs=[
            pl.BlockSpec((4, HEAD_DIM), lambda bi, hi, qi: (0, 0)),
            pl.BlockSpec((1, g, V_DIM, t), lambda bi, hi, qi: (bi, hi, 0, qi)),
            pl.BlockSpec((1, g, V_DIM, t), lambda bi, hi, qi: (bi, hi, 0, jnp.minimum(qi + 1, nq - 1))),
            pl.BlockSpec((1, s, g * V_DIM), lambda bi, hi, qi: (bi, 0, hi)),
            pl.BlockSpec((1, g, nq, V_DIM, t), lambda bi, hi, qi: (bi, hi, 0, 0, 0)),
            pl.BlockSpec((t, V_DIM), lambda bi, hi, qi: (0, 0)),
            pl.BlockSpec((g, V_DIM, t), lambda bi, hi, qi: (hi, 0, 0)),
            pl.BlockSpec((t, t), lambda bi, hi, qi: (0, 0)),
            pl.BlockSpec((g, 8, t), lambda bi, hi, qi: (hi, 0, 0)),
            pl.BlockSpec((V_DIM, 1), lambda bi, hi, qi: (0, 0)),
        ],
        out_specs=pl.BlockSpec((1, g, V_DIM, t), lambda bi, hi, qi: (bi, hi, 0, qi)),
        out_shape=jax.ShapeDtypeStruct((b, h, V_DIM, s), BF16),
        scratch_shapes=[
            pltpu.VMEM((2, g, 2, 2 * V_DIM, t), BF16),
            pltpu.VMEM((2 * g, t, t), F32),
            pltpu.VMEM((2 * g, 8, t), F32),
            pltpu.VMEM((2 * g, ACC_ROWS, t), F32),
        ],
        compiler_params=pltpu.CompilerParams(
            dimension_semantics=("parallel", "parallel", "arbitrary"), vmem_limit_bytes=VMEM_LIMIT),
        name="attention",
    )(lamp, qT, qT, k, vT5, kext, qext, mask, slope, gsub)


def _merge_ffn_kernel(x_ref, attT_ref, rnn_ref, gx_ref, wg_ref, bg_ref, wa_ref, wr_ref, wo_ref,
                      gm_ref, w1_ref, w2_ref, gf_ref, o_ref):
    x = x_ref[0]
    h = _rms(x, gx_ref[...]).astype(BF16)
    y_att = lax.dot_general(attT_ref[0], wa_ref[...], (((0,), (0,)), ((), ())),
                            preferred_element_type=F32)
    y_rnn = jnp.dot(rnn_ref[0], wr_ref[...], preferred_element_type=F32)
    gts = jax.nn.sigmoid(jnp.dot(h, wg_ref[...], preferred_element_type=F32) + bg_ref[...])
    m = gts[:, :D_MODEL] * y_att + gts[:, D_MODEL:] * y_rnn
    x1 = x + jnp.dot(m.astype(BF16), wo_ref[...], preferred_element_type=F32)
    h2 = _rms(x1, gm_ref[...]).astype(BF16)
    f = jnp.dot(h2, w1_ref[...], preferred_element_type=F32)
    f = jnp.square(jnp.maximum(f, 0.0)).astype(BF16)
    x2 = x1 + jnp.dot(f, w2_ref[...], preferred_element_type=F32)
    o_ref[0] = _rms(x2, gf_ref[...])


def _merge_ffn(x, attT, rnn, g_mix, w_gate, b_gate, w_att, w_rnn, w_o, g_mlp, w1, w2, g_final):
    b, s, d = x.shape
    tm = FFN_TM
    const = lambda shape: pl.BlockSpec(shape, lambda bi, si: (0,) * len(shape),
                                       pipeline_mode=pl.Buffered(1))
    return pl.pallas_call(
        _merge_ffn_kernel,
        grid=(b, s // tm),
        in_specs=[
            pl.BlockSpec((1, tm, d), lambda bi, si: (bi, si, 0)),
            pl.BlockSpec((1, D_V, tm), lambda bi, si: (bi, 0, si)),
            pl.BlockSpec((1, tm, D_RNN), lambda bi, si: (bi, si, 0)),
            const((1, d)), const((d, 2 * d)),
            const((1, 2 * d)),
            const((D_V, d)), const((D_RNN, d)), const((d, d)),
            const((1, d)),
            const((d, D_FF)), const((D_FF, d)),
            const((1, d)),
        ],
        out_specs=pl.BlockSpec((1, tm, d), lambda bi, si: (bi, si, 0)),
        out_shape=jax.ShapeDtypeStruct((b, s, d), F32),
        compiler_params=pltpu.CompilerParams(
            dimension_semantics=("parallel", "parallel"), vmem_limit_bytes=VMEM_LIMIT),
        name="merge_ffn",
    )(x, attT, rnn, g_mix, w_gate, b_gate, w_att, w_rnn, w_o, g_mlp, w1, w2, g_final)


def _split_bf16(v, pieces=3):
    out, r = [], np.asarray(v, np.float64)
    for _ in range(pieces):
        piece = r.astype(np.float32).astype(BF16)
        out.append(piece)
        r = r - piece.astype(np.float64)
    return out


def _alibi_tables(tile):
    slopes = 2.0 ** (-8.0 * np.arange(1, N_HEADS + 1, dtype=np.float64) / N_HEADS) * LOG2E
    kr = np.arange(tile, dtype=np.float64)
    qr = np.arange(tile, dtype=np.float64)
    assert tile <= 256 * 256
    kext = np.zeros((tile, V_DIM), np.float32)
    kext[:, 0:3] = np.mod(kr, 256.0)[:, None]
    kext[:, 3:6] = np.floor(kr / 256.0)[:, None]
    kext[:, 6:9] = 1.0
    qext = np.zeros((N_HEADS, V_DIM, tile), BF16)
    for n, piece in enumerate(_split_bf16(slopes)):
        qext[:, n, :] = piece[:, None]
        qext[:, 3 + n, :] = (piece.astype(np.float32) * 256.0).astype(BF16)[:, None]
    for n, piece in enumerate(_split_bf16(-slopes[:, None] * qr[None, :])):
        qext[:, 6 + n, :] = piece
    mask = np.where(kr[:, None] > qr[None, :], NEG_BIG, 0.0).astype(np.float32)
    slope_rows = np.broadcast_to((slopes * tile).astype(np.float32)[:, None, None], (N_HEADS, 8, tile))
    return jnp.asarray(kext, BF16), jnp.asarray(qext), jnp.asarray(mask), jnp.asarray(slope_rows)


def kernel(x, w_in, b_gate, g_mix, lambda_q1, lambda_k1, lambda_q2, lambda_k2, subln_g,
           conv_w, conv_b, w_r, b_r, w_i, b_i, lru_lambda, w_att_out, w_rnn_out, w_o,
           g_mlp, w_ff1, w_ff2, g_final):
    b, s, d = x.shape
    assert (d, w_in.shape[0]) == (D_MODEL, 1) and s % ATT_TILE == 0
    t = ATT_TILE
    w_in_b = w_in[0].astype(BF16)
    w_q, w_k, w_v = (w_in_b[:, n * D_QK:(n + 1) * D_QK] for n in range(3))
    w_ri = jnp.concatenate([w_r[0], w_i[0]], axis=-1).astype(BF16)
    qT, k, vT5, rnn = _qkv_rglru(x, g_mix, w_q.T, w_k, w_v.T, w_in_b[:, D_QKV:D_QKV + 2 * D_RNN],
                                 conv_w[0], conv_b, w_ri, b_r, b_i, lru_lambda)
    kext, qext, mask, slope_rows = _alibi_tables(t)
    lamp = jnp.concatenate([lambda_q1, lambda_k1, lambda_q2, lambda_k2], axis=0).astype(F32)
    attT = _attention(lamp, qT, k, vT5, kext, qext, mask, slope_rows, subln_g.reshape(V_DIM, 1))
    attT = attT.reshape(b, D_V, s)

    return _merge_ffn(x, attT, rnn, g_mix, w_in_b[:, D_QKV + 2 * D_RNN:], b_gate,
                      w_att_out[0].astype(BF16), w_rnn_out[0].astype(BF16), w_o[0].astype(BF16),
                      g_mlp, w_ff1[0].astype(BF16), w_ff2[0].astype(BF16), g_final.reshape(1, d))
```

```python
import math

import jax
import jax.numpy as jnp
import numpy as np
from jax import lax
from jax.experimental import pallas as pl
from jax.experimental.pallas import tpu as pltpu

D_MODEL = 1024
N_HEADS = 8
HEAD_DIM = 64
V_DIM = 2 * HEAD_DIM
ACC_ROWS = V_DIM + 16
D_QK = N_HEADS * 2 * HEAD_DIM
D_V = N_HEADS * V_DIM
D_RNN = D_MODEL
N_RNN_BLOCKS = 8
RNN_BLOCK = D_RNN // N_RNN_BLOCKS
CONV_WIDTH = 4
LRU_C = 8.0
D_FF = 4 * D_MODEL
D_QKV = 2 * D_QK + D_V
EPS = 1e-6
LAM_INIT = 0.8 - 0.6 * math.exp(-0.3 * 0)
LOG2E = 1.4426950408889634
NEG_BIG = -1e30
TINY = 1e-30

ATT_TILE = 512
ATT_HEADS_PER_STEP = 4
FFN_TM = 512
VMEM_LIMIT = 56 * 1024 * 1024

F32 = jnp.float32
BF16 = jnp.bfloat16


def _rms(x, g):
    return x * lax.rsqrt(jnp.mean(x * x, axis=-1, keepdims=True) + EPS) * g


def _zero_row(v):
    zero = jnp.minimum(jnp.abs(v[0:1, 0:RNN_BLOCK]), 0.0)
    return jnp.concatenate([zero] * (D_RNN // RNN_BLOCK), axis=1)


def _qkv_rglru_kernel(x_ref, g_ref, wqT_ref, wk_ref, wvT_ref, wxg_ref, cw_ref, cb_ref, wri_ref, br_ref, bi_ref,
                      lam_ref, qT_ref, k_ref, vT_ref, o_ref, xg_sc, o_sc, ch_sc, al_sc, ul_sc, c_sc):
    ts = x_ref.shape[1]
    half = ts // 2
    nr = half // 8
    nb = D_RNN // RNN_BLOCK
    sblk = pl.program_id(1)
    nt = (((1,), (1,)), ((), ()))

    @pl.when(sblk == 0)
    def _():
        xg_sc[0:nb, 0:8, :] = jnp.zeros((nb, 8, RNN_BLOCK), F32)
        ch_sc[...] = jnp.zeros(ch_sc.shape, F32)

    h = _rms(x_ref[0], g_ref[...]).astype(BF16)
    for base in (0, half):
        xg = jnp.dot(h[base:base + half], wxg_ref[...], preferred_element_type=F32)
        for c in range(2 * nb):
            xg_sc[c, 8 + base:8 + base + half, :] = xg[:, c * RNN_BLOCK:(c + 1) * RNN_BLOCK]

    nl = -lam_ref[...]
    sp = jnp.maximum(nl, 0.0) + jnp.log(1.0 + jnp.exp(-jnp.abs(nl)))
    cw = cw_ref[...]
    row = lax.broadcasted_iota(jnp.int32, (half, D_RNN), 0)

    def phase(col0, base, s):
        return jnp.concatenate([xg_sc[col0 + c, pl.ds(8 + base + s, nr, stride=8), :] for c in range(nb)], axis=1)

    def conv_gates(base):
        xph = {s: phase(0, base, s) for s in range(1 - CONV_WIDTH, 8)}
        xc = jnp.concatenate(
            [cb_ref[...] + sum(cw[CONV_WIDTH - 1 - d:CONV_WIDTH - d] * xph[s - d] for d in range(CONV_WIDTH))
             for s in range(8)], axis=0)
        xcb = xc.astype(BF16)
        r_parts, i_parts = [], []
        for n in range(N_RNN_BLOCKS):
            ri = jnp.dot(xcb[:, n * RNN_BLOCK:(n + 1) * RNN_BLOCK], wri_ref[n], preferred_element_type=F32)
            r_parts.append(ri[:, :RNN_BLOCK])
            i_parts.append(ri[:, RNN_BLOCK:])
        return xc, jnp.concatenate(r_parts, axis=1), jnp.concatenate(i_parts, axis=1)

    def recurrence(base, xc, r_pre, i_pre, c, first, z_gate=0.0, z_scan=0.0, z_out=0.0):
        r = jax.nn.sigmoid(r_pre + (br_ref[...] + z_gate))
        ig = jax.nn.sigmoid(i_pre + bi_ref[...])
        a = jnp.exp2((-LRU_C * LOG2E) * sp * r)
        y = 1.0 - a * a
        mult = y * lax.rsqrt(jnp.maximum(y, TINY))
        if first:
            mult = jnp.where(jnp.logical_and(row == 0, sblk == 0), 1.0, mult)
        u = mult * (ig * xc)
        hloc, ploc = [u[0:nr] + z_scan], [a[0:nr]]
        for s in range(1, 8):
            a_s = a[s * nr:(s + 1) * nr]
            hloc.append(a_s * hloc[-1] + u[s * nr:(s + 1) * nr])
            ploc.append(a_s * ploc[-1])
        ul_sc[...] = hloc[7]
        al_sc[...] = ploc[7]
        for rr in range(nr):
            c_sc[rr:rr + 1, :] = c
            c = ul_sc[rr:rr + 1, :] + al_sc[rr:rr + 1, :] * c
        c_in = c_sc[...]
        for s in range(8):
            out = (hloc[s] + ploc[s] * c_in) * jax.nn.gelu(phase(nb, base, s))
            if s == 7:
                out = out + z_out
            for cb in range(nb):
                o_sc[cb, pl.ds(base + s, nr, stride=8), :] = out[:, cb * RNN_BLOCK:(cb + 1) * RNN_BLOCK]
        return c

    xc_a, r_a, i_a = conv_gates(0)
    qT = lax.dot_general(wqT_ref[...], h, nt, preferred_element_type=F32)
    qT_ref[0] = (qT * (HEAD_DIM ** -0.5 * LOG2E)).astype(BF16).reshape(N_HEADS, V_DIM, ts)
    c_mid = recurrence(0, xc_a, r_a, i_a, ch_sc[...], True)

    xc_b, r_b, i_b = conv_gates(half)
    kk = jnp.dot(h, wk_ref[...], preferred_element_type=F32)
    k_ref[0] = kk.astype(BF16)
    vT = lax.dot_general(wvT_ref[...], h, nt, preferred_element_type=F32)
    vT_ref[0, :, 0] = vT.astype(BF16).reshape(N_HEADS, V_DIM, ts)
    ch_sc[...] = recurrence(half, xc_b, r_b, i_b, c_mid, False,
                            z_gate=_zero_row(qT), z_scan=_zero_row(kk), z_out=_zero_row(vT))

    for c in range(nb):
        xg_sc[c, 0:8, :] = xg_sc[c, ts:ts + 8, :]
    o_ref[0] = jnp.concatenate([o_sc[c] for c in range(nb)], axis=1).astype(o_ref.dtype)


def _qkv_rglru(x, g, wqT, wk, wvT, w_xg, conv_w, conv_b, w_ri, b_r, b_i, lru_lambda):
    b, s, d = x.shape
    ts = ATT_TILE
    nb = D_RNN // RNN_BLOCK
    const = lambda shape: pl.BlockSpec(shape, lambda bi, si: (0,) * len(shape))
    return pl.pallas_call(
        _qkv_rglru_kernel,
        grid=(b, s // ts),
        in_specs=[
            pl.BlockSpec((1, ts, d), lambda bi, si: (bi, si, 0)),
            const((1, d)), const((D_QK, d)), const((d, D_QK)), const((D_V, d)),
            const((d, 2 * D_RNN)), const((CONV_WIDTH, D_RNN)), const((1, D_RNN)),
            const((N_RNN_BLOCKS, RNN_BLOCK, 2 * RNN_BLOCK)),
            const((1, D_RNN)), const((1, D_RNN)), const((1, D_RNN)),
        ],
        out_specs=[
            pl.BlockSpec((1, N_HEADS, V_DIM, ts), lambda bi, si: (bi, 0, 0, si)),
            pl.BlockSpec((1, ts, D_QK), lambda bi, si: (bi, si, 0)),
            pl.BlockSpec((1, N_HEADS, 1, V_DIM, ts), lambda bi, si: (bi, 0, si, 0, 0)),
            pl.BlockSpec((1, ts, D_RNN), lambda bi, si: (bi, si, 0)),
        ],
        out_shape=[
            jax.ShapeDtypeStruct((b, N_HEADS, V_DIM, s), BF16),
            jax.ShapeDtypeStruct((b, s, D_QK), BF16),
            jax.ShapeDtypeStruct((b, N_HEADS, s // ts, V_DIM, ts), BF16),
            jax.ShapeDtypeStruct((b, s, D_RNN), BF16),
        ],
        scratch_shapes=[
            pltpu.VMEM((2 * nb, ts + 8, RNN_BLOCK), F32),
            pltpu.VMEM((nb, ts, RNN_BLOCK), F32),
            pltpu.VMEM((1, D_RNN), F32),
            pltpu.VMEM((ts // 16, D_RNN), F32),
            pltpu.VMEM((ts // 16, D_RNN), F32),
            pltpu.VMEM((ts // 16, D_RNN), F32),
        ],
        compiler_params=pltpu.CompilerParams(
            dimension_semantics=("parallel", "arbitrary"), vmem_limit_bytes=VMEM_LIMIT),
        name="qkv_rglru",
    )(x, g, wqT, wk, wvT, w_xg, conv_w, conv_b, w_ri, b_r, b_i, lru_lambda)


def _sublane_allmax(v):
    for sh in (4, 2, 1):
        v = jnp.maximum(v, pltpu.roll(v, sh, 0))
    return v


def _attn_kernel(lamp_ref, qT_ref, qTn_ref, k_ref, vT_ref, kext_ref, qext_ref, mask_ref, slope_ref, gsub_ref,
                 o_ref, qz_sc, s_sc, m_sc, acc_sc):
    g_heads = qT_ref.shape[1]
    tile = qT_ref.shape[3]
    i = pl.program_id(2)
    chains = [(g, c) for g in range(g_heads) for c in range(2)]
    nch = len(chains)
    ones_rows = jnp.ones((ACC_ROWS - V_DIM, tile), BF16)
    cur = jnp.bitwise_and(i, 1)

    def build_qz(src_ref, slot):
        row = lax.broadcasted_iota(jnp.int32, (V_DIM, tile), 0)
        for g in range(g_heads):
            q_both = src_ref[0, g].astype(F32)
            qz_sc[slot, g, 0, 0:V_DIM] = jnp.where(row < HEAD_DIM, q_both, 0.0).astype(BF16)
            qz_sc[slot, g, 1, 0:V_DIM] = jnp.where(row >= HEAD_DIM, q_both, 0.0).astype(BF16)
            qz_sc[slot, g, 0, V_DIM:2 * V_DIM] = qext_ref[g]
            qz_sc[slot, g, 1, V_DIM:2 * V_DIM] = qext_ref[g]

    def scores(slot, j, idx):
        g, c = chains[idx]
        kt = k_ref[0, pl.ds(pl.multiple_of(j * tile, tile), tile), g * V_DIM:(g + 1) * V_DIM]
        lhs = jnp.concatenate([kt, kext_ref[...]], axis=1)
        return jnp.dot(lhs, qz_sc[slot, g, c], preferred_element_type=F32)

    @pl.when(i == 0)
    def _():
        build_qz(qT_ref, 0)
        for idx in range(nch):
            s_sc[idx] = scores(0, 0, idx)

    m_sc[...] = jnp.full(m_sc.shape, NEG_BIG, F32)
    acc_sc[...] = jnp.zeros(acc_sc.shape, F32)

    def update_acc(idx, alpha, pv):
        acc = acc_sc[idx].reshape(ACC_ROWS // 8, 8, tile) * alpha[None] + pv.reshape(ACC_ROWS // 8, 8, tile)
        acc_sc[idx] = acc.reshape(ACC_ROWS, tile)

    def step(j, diagonal):
        dist = jnp.full((8, tile), i - j, jnp.int32).astype(F32)
        pending = None
        for idx, (g, c) in enumerate(chains):
            s = s_sc[idx]
            s_next = scores(1 - cur, 0, idx) if diagonal else scores(cur, j + 1, idx)
            dvec = slope_ref[g] * dist
            if diagonal:
                s = s + mask_ref[...]
            t = s.reshape(tile // 8, 8, tile)
            m_old = m_sc[idx]
            m_new = jnp.maximum(m_old, _sublane_allmax(jnp.max(t, axis=0)) - dvec)
            p = jnp.exp2(t - (m_new + dvec)[None])
            alpha = jnp.exp2(m_old - m_new)
            m_sc[idx] = m_new
            lhs = jnp.concatenate([vT_ref[0, g, j], ones_rows], axis=0)
            pv = jnp.dot(lhs, p.reshape(tile, tile).astype(BF16), preferred_element_type=F32)
            s_sc[idx] = s_next
            if pending is not None:
                update_acc(*pending)
            pending = (idx, alpha, pv)
        update_acc(*pending)

    def body(j, carry):
        step(j, False)
        return carry

    lax.fori_loop(0, i, body, 0)
    build_qz(qTn_ref, 1 - cur)
    step(i, True)

    lp = lamp_ref[...]
    lam = (jnp.exp(jnp.sum(lp[0:1] * lp[1:2], axis=-1, keepdims=True))
           - jnp.exp(jnp.sum(lp[2:3] * lp[3:4], axis=-1, keepdims=True)) + LAM_INIT)
    for g in range(g_heads):
        a1 = acc_sc[2 * g]
        a2 = acc_sc[2 * g + 1]
        o1 = a1[0:V_DIM] / a1[V_DIM:V_DIM + 1]
        o2 = a2[0:V_DIM] / a2[V_DIM:V_DIM + 1]
        a = o1 - lam * o2
        ms = jnp.mean(a * a, axis=0, keepdims=True)
        y = a * lax.rsqrt(ms + EPS) * gsub_ref[...] * (1.0 - LAM_INIT)
        o_ref[0, g] = y.astype(o_ref.dtype)


def _attention(lamp, qT, k, vT5, kext, qext, mask, slope, gsub):
    b, h, _, s = qT.shape
    g = ATT_HEADS_PER_STEP
    t = ATT_TILE
    nq = s // t
    return pl.pallas_call(
        _attn_kernel,
        grid=(b, h // g, nq),
        in_specs=[
            pl.BlockSpec((4, HEAD_DIM), lambda bi, hi, qi: (0, 0)),
            pl.BlockSpec((1, g, V_DIM, t), lambda bi, hi, qi: (bi, hi, 0, qi)),
            pl.BlockSpec((1, g, V_DIM, t), lambda bi, hi, qi: (bi, hi, 0, jnp.minimum(qi + 1, nq - 1))),
            pl.BlockSpec((1, s, g * V_DIM), lambda bi, hi, qi: (bi, 0, hi)),
            pl.BlockSpec((1, g, nq, V_DIM, t), lambda bi, hi, qi: (bi, hi, 0, 0, 0)),
            pl.BlockSpec((t, V_DIM), lambda bi, hi, qi: (0, 0)),
            pl.BlockSpec((g, V_DIM, t), lambda bi, hi, qi: (hi, 0, 0)),
            pl.BlockSpec((t, t), lambda bi, hi, qi: (0, 0)),
            pl.BlockSpec((g, 8, t), lambda bi, hi, qi: (hi, 0, 0)),
            pl.BlockSpec((V_DIM, 1), lambda bi, hi, qi: (0, 0)),
        ],
        out_specs=pl.BlockSpec((1, g, V_DIM, t), lambda bi, hi, qi: (bi, hi, 0, qi)),
        out_shape=jax.ShapeDtypeStruct((b, h, V_DIM, s), BF16),
        scratch_shapes=[
            pltpu.VMEM((2, g, 2, 2 * V_DIM, t), BF16),
            pltpu.VMEM((2 * g, t, t), F32),
            pltpu.VMEM((2 * g, 8, t), F32),
            pltpu.VMEM((2 * g, ACC_ROWS, t), F32),
        ],
        compiler_params=pltpu.CompilerParams(
            dimension_semantics=("parallel", "parallel", "arbitrary"), vmem_limit_bytes=VMEM_LIMIT),
        name="attention",
    )(lamp, qT, qT, k, vT5, kext, qext, mask, slope, gsub)


def _merge_ffn_kernel(x_ref, attT_ref, rnn_ref, gx_ref, wg_ref, bg_ref, wa_ref, wr_ref, wo_ref,
                      gm_ref, w1_ref, w2_ref, gf_ref, o_ref):
    x = x_ref[0]
    h = _rms(x, gx_ref[...]).astype(BF16)
    y_att = lax.dot_general(attT_ref[0], wa_ref[...], (((0,), (0,)), ((), ())),
                            preferred_element_type=F32)
    y_rnn = jnp.dot(rnn_ref[0], wr_ref[...], preferred_element_type=F32)
    gts = jax.nn.sigmoid(jnp.dot(h, wg_ref[...], preferred_element_type=F32) + bg_ref[...])
    m = gts[:, :D_MODEL] * y_att + gts[:, D_MODEL:] * y_rnn
    x1 = x + jnp.dot(m.astype(BF16), wo_ref[...], preferred_element_type=F32)
    h2 = _rms(x1, gm_ref[...]).astype(BF16)
    f = jnp.dot(h2, w1_ref[...], preferred_element_type=F32)
    f = jnp.square(jnp.maximum(f, 0.0)).astype(BF16)
    x2 = x1 + jnp.dot(f, w2_ref[...], preferred_element_type=F32)
    o_ref[0] = _rms(x2, gf_ref[...])


def _merge_ffn(x, attT, rnn, g_mix, w_gate, b_gate, w_att, w_rnn, w_o, g_mlp, w1, w2, g_final):
    b, s, d = x.shape
    tm = FFN_TM
    const = lambda shape: pl.BlockSpec(shape, lambda bi, si: (0,) * len(shape),
                                       pipeline_mode=pl.Buffered(1))
    return pl.pallas_call(
        _merge_ffn_kernel,
        grid=(b, s // tm),
        in_specs=[
            pl.BlockSpec((1, tm, d), lambda bi, si: (bi, si, 0)),
            pl.BlockSpec((1, D_V, tm), lambda bi, si: (bi, 0, si)),
            pl.BlockSpec((1, tm, D_RNN), lambda bi, si: (bi, si, 0)),
            const((1, d)), const((d, 2 * d)),
            const((1, 2 * d)),
            const((D_V, d)), const((D_RNN, d)), const((d, d)),
            const((1, d)),
            const((d, D_FF)), const((D_FF, d)),
            const((1, d)),
        ],
        out_specs=pl.BlockSpec((1, tm, d), lambda bi, si: (bi, si, 0)),
        out_shape=jax.ShapeDtypeStruct((b, s, d), F32),
        compiler_params=pltpu.CompilerParams(
            dimension_semantics=("parallel", "parallel"), vmem_limit_bytes=VMEM_LIMIT),
        name="merge_ffn",
    )(x, attT, rnn, g_mix, w_gate, b_gate, w_att, w_rnn, w_o, g_mlp, w1, w2, g_final)


def _split_bf16(v, pieces=3):
    out, r = [], np.asarray(v, np.float64)
    for _ in range(pieces):
        piece = r.astype(np.float32).astype(BF16)
        out.append(piece)
        r = r - piece.astype(np.float64)
    return out


def _alibi_tables(tile):
    slopes = 2.0 ** (-8.0 * np.arange(1, N_HEADS + 1, dtype=np.float64) / N_HEADS) * LOG2E
    kr = np.arange(tile, dtype=np.float64)
    qr = np.arange(tile, dtype=np.float64)
    assert tile <= 256 * 256
    kext = np.zeros((tile, V_DIM), np.float32)
    kext[:, 0:3] = np.mod(kr, 256.0)[:, None]
    kext[:, 3:6] = np.floor(kr / 256.0)[:, None]
    kext[:, 6:9] = 1.0
    qext = np.zeros((N_HEADS, V_DIM, tile), BF16)
    for n, piece in enumerate(_split_bf16(slopes)):
        qext[:, n, :] = piece[:, None]
        qext[:, 3 + n, :] = (piece.astype(np.float32) * 256.0).astype(BF16)[:, None]
    for n, piece in enumerate(_split_bf16(-slopes[:, None] * qr[None, :])):
        qext[:, 6 + n, :] = piece
    mask = np.where(kr[:, None] > qr[None, :], NEG_BIG, 0.0).astype(np.float32)
    slope_rows = np.broadcast_to((slopes * tile).astype(np.float32)[:, None, None], (N_HEADS, 8, tile))
    return jnp.asarray(kext, BF16), jnp.asarray(qext), jnp.asarray(mask), jnp.asarray(slope_rows)


def kernel(x, w_in, b_gate, g_mix, lambda_q1, lambda_k1, lambda_q2, lambda_k2, subln_g,
           conv_w, conv_b, w_r, b_r, w_i, b_i, lru_lambda, w_att_out, w_rnn_out, w_o,
           g_mlp, w_ff1, w_ff2, g_final):
    b, s, d = x.shape
    assert (d, w_in.shape[0]) == (D_MODEL, 1) and s % ATT_TILE == 0
    t = ATT_TILE
    w_in_b = w_in[0].astype(BF16)
    w_q, w_k, w_v = (w_in_b[:, n * D_QK:(n + 1) * D_QK] for n in range(3))
    w_ri = jnp.concatenate([w_r[0], w_i[0]], axis=-1).astype(BF16)
    qT, k, vT5, rnn = _qkv_rglru(x, g_mix, w_q.T, w_k, w_v.T, w_in_b[:, D_QKV:D_QKV + 2 * D_RNN],
                                 conv_w[0], conv_b, w_ri, b_r, b_i, lru_lambda)
    kext, qext, mask, slope_rows = _alibi_tables(t)
    lamp = jnp.concatenate([lambda_q1, lambda_k1, lambda_q2, lambda_k2], axis=0).astype(F32)
    attT = _attention(lamp, qT, k, vT5, kext, qext, mask, slope_rows, subln_g.reshape(V_DIM, 1))
    attT = attT.reshape(b, D_V, s)

    return _merge_ffn(x, attT, rnn, g_mix, w_in_b[:, D_QKV + 2 * D_RNN:], b_gate,
                      w_att_out[0].astype(BF16), w_rnn_out[0].astype(BF16), w_o[0].astype(BF16),
                      g_mlp, w_ff1[0].astype(BF16), w_ff2[0].astype(BF16), g_final.reshape(1, d))
```

```python
import math

import jax
import jax.numpy as jnp
import numpy as np
from jax import lax
from jax.experimental import pallas as pl
from jax.experimental.pallas import tpu as pltpu

D_MODEL = 1024
N_HEADS = 8
HEAD_DIM = 64
V_DIM = 2 * HEAD_DIM
ACC_ROWS = V_DIM + 16
D_QK = N_HEADS * 2 * HEAD_DIM
D_V = N_HEADS * V_DIM
D_RNN = D_MODEL
N_RNN_BLOCKS = 8
RNN_BLOCK = D_RNN // N_RNN_BLOCKS
CONV_WIDTH = 4
LRU_C = 8.0
D_FF = 4 * D_MODEL
D_QKV = 2 * D_QK + D_V
EPS = 1e-6
LAM_INIT = 0.8 - 0.6 * math.exp(-0.3 * 0)
LOG2E = 1.4426950408889634
NEG_BIG = -1e30
TINY = 1e-30

ATT_TILE = 512
ATT_HEADS_PER_STEP = 4
FFN_TM = 512
VMEM_LIMIT = 56 * 1024 * 1024

F32 = jnp.float32
BF16 = jnp.bfloat16


def _rms(x, g):
    return x * lax.rsqrt(jnp.mean(x * x, axis=-1, keepdims=True) + EPS) * g


def _zero_row(v):
    zero = jnp.minimum(jnp.abs(v[0:1, 0:RNN_BLOCK]), 0.0)
    return jnp.concatenate([zero] * (D_RNN // RNN_BLOCK), axis=1)


def _qkv_rglru_kernel(x_ref, g_ref, wqT_ref, wk_ref, wvT_ref, wxg_ref, cw_ref, cb_ref, wri_ref, br_ref, bi_ref,
                      lam_ref, qT_ref, k_ref, vT_ref, o_ref, xg_sc, o_sc, ch_sc, al_sc, ul_sc, c_sc):
    ts = x_ref.shape[1]
    half = ts // 2
    nr = half // 8
    nb = D_RNN // RNN_BLOCK
    sblk = pl.program_id(1)
    nt = (((1,), (1,)), ((), ()))

    @pl.when(sblk == 0)
    def _():
        xg_sc[0:nb, 0:8, :] = jnp.zeros((nb, 8, RNN_BLOCK), F32)
        ch_sc[...] = jnp.zeros(ch_sc.shape, F32)

    h = _rms(x_ref[0], g_ref[...]).astype(BF16)
    for base in (0, half):
        xg = jnp.dot(h[base:base + half], wxg_ref[...], preferred_element_type=F32)
        for c in range(2 * nb):
            xg_sc[c, 8 + base:8 + base + half, :] = xg[:, c * RNN_BLOCK:(c + 1) * RNN_BLOCK]

    nl = -lam_ref[...]
    sp = jnp.maximum(nl, 0.0) + jnp.log(1.0 + jnp.exp(-jnp.abs(nl)))
    cw = cw_ref[...]
    row = lax.broadcasted_iota(jnp.int32, (half, D_RNN), 0)

    def phase(col0, base, s):
        return jnp.concatenate([xg_sc[col0 + c, pl.ds(8 + base + s, nr, stride=8), :] for c in range(nb)], axis=1)

    def conv_gates(base):
        xph = {s: phase(0, base, s) for s in range(1 - CONV_WIDTH, 8)}
        xc = jnp.concatenate(
            [cb_ref[...] + sum(cw[CONV_WIDTH - 1 - d:CONV_WIDTH - d] * xph[s - d] for d in range(CONV_WIDTH))
             for s in range(8)], axis=0)
        xcb = xc.astype(BF16)
        r_parts, i_parts = [], []
        for n in range(N_RNN_BLOCKS):
            ri = jnp.dot(xcb[:, n * RNN_BLOCK:(n + 1) * RNN_BLOCK], wri_ref[n], preferred_element_type=F32)
            r_parts.append(ri[:, :RNN_BLOCK])
            i_parts.append(ri[:, RNN_BLOCK:])
        return xc, jnp.concatenate(r_parts, axis=1), jnp.concatenate(i_parts, axis=1)

    def recurrence(base, xc, r_pre, i_pre, c, first, z_gate=0.0, z_scan=0.0, z_out=0.0):
        r = jax.nn.sigmoid(r_pre + (br_ref[...] + z_gate))
        ig = jax.nn.sigmoid(i_pre + bi_ref[...])
        a = jnp.exp2((-LRU_C * LOG2E) * sp * r)
        y = 1.0 - a * a
        mult = y * lax.rsqrt(jnp.maximum(y, TINY))
        if first:
            mult = jnp.where(jnp.logical_and(row == 0, sblk == 0), 1.0, mult)
        u = mult * (ig * xc)
        hloc, ploc = [u[0:nr] + z_scan], [a[0:nr]]
        for s in range(1, 8):
            a_s = a[s * nr:(s + 1) * nr]
            hloc.append(a_s * hloc[-1] + u[s * nr:(s + 1) * nr])
            ploc.append(a_s * ploc[-1])
        ul_sc[...] = hloc[7]
        al_sc[...] = ploc[7]
        for rr in range(nr):
            c_sc[rr:rr + 1, :] = c
            c = ul_sc[rr:rr + 1, :] + al_sc[rr:rr + 1, :] * c
        c_in = c_sc[...]
        for s in range(8):
            out = (hloc[s] + ploc[s] * c_in) * jax.nn.gelu(phase(nb, base, s))
            if s == 7:
                out = out + z_out
            for cb in range(nb):
                o_sc[cb, pl.ds(base + s, nr, stride=8), :] = out[:, cb * RNN_BLOCK:(cb + 1) * RNN_BLOCK]
        return c

    xc_a, r_a, i_a = conv_gates(0)
    qT = lax.dot_general(wqT_ref[...], h, nt, preferred_element_type=F32)
    qT_ref[0] = (qT * (HEAD_DIM ** -0.5 * LOG2E)).astype(BF16).reshape(N_HEADS, V_DIM, ts)
    c_mid = recurrence(0, xc_a, r_a, i_a, ch_sc[...], True)

    xc_b, r_b, i_b = conv_gates(half)
    kk = jnp.dot(h, wk_ref[...], preferred_element_type=F32)
    k_ref[0] = kk.astype(BF16)
    vT = lax.dot_general(wvT_ref[...], h, nt, preferred_element_type=F32)
    vT_ref[0, :, 0] = vT.astype(BF16).reshape(N_HEADS, V_DIM, ts)
    ch_sc[...] = recurrence(half, xc_b, r_b, i_b, c_mid, False,
                            z_gate=_zero_row(qT), z_scan=_zero_row(kk), z_out=_zero_row(vT))

    for c in range(nb):
        xg_sc[c, 0:8, :] = xg_sc[c, ts:ts + 8, :]
    o_ref[0] = jnp.concatenate([o_sc[c] for c in range(nb)], axis=1).astype(o_ref.dtype)


def _qkv_rglru(x, g, wqT, wk, wvT, w_xg, conv_w, conv_b, w_ri, b_r, b_i, lru_lambda):
    b, s, d = x.shape
    ts = ATT_TILE
    nb = D_RNN // RNN_BLOCK
    const = lambda shape: pl.BlockSpec(shape, lambda bi, si: (0,) * len(shape))
    return pl.pallas_call(
        _qkv_rglru_kernel,
        grid=(b, s // ts),
        in_specs=[
            pl.BlockSpec((1, ts, d), lambda bi, si: (bi, si, 0)),
            const((1, d)), const((D_QK, d)), const((d, D_QK)), const((D_V, d)),
            const((d, 2 * D_RNN)), const((CONV_WIDTH, D_RNN)), const((1, D_RNN)),
            const((N_RNN_BLOCKS, RNN_BLOCK, 2 * RNN_BLOCK)),
            const((1, D_RNN)), const((1, D_RNN)), const((1, D_RNN)),
        ],
        out_specs=[
            pl.BlockSpec((1, N_HEADS, V_DIM, ts), lambda bi, si: (bi, 0, 0, si)),
            pl.BlockSpec((1, ts, D_QK), lambda bi, si: (bi, si, 0)),
            pl.BlockSpec((1, N_HEADS, 1, V_DIM, ts), lambda bi, si: (bi, 0, si, 0, 0)),
            pl.BlockSpec((1, ts, D_RNN), lambda bi, si: (bi, si, 0)),
        ],
        out_shape=[
            jax.ShapeDtypeStruct((b, N_HEADS, V_DIM, s), BF16),
            jax.ShapeDtypeStruct((b, s, D_QK), BF16),
            jax.ShapeDtypeStruct((b, N_HEADS, s // ts, V_DIM, ts), BF16),
            jax.ShapeDtypeStruct((b, s, D_RNN), BF16),
        ],
        scratch_shapes=[
            pltpu.VMEM((2 * nb, ts + 8, RNN_BLOCK), F32),
            pltpu.VMEM((nb, ts, RNN_BLOCK), F32),
            pltpu.VMEM((1, D_RNN), F32),
            pltpu.VMEM((ts // 16, D_RNN), F32),
            pltpu.VMEM((ts // 16, D_RNN), F32),
            pltpu.VMEM((ts // 16, D_RNN), F32),
        ],
        compiler_params=pltpu.CompilerParams(
            dimension_semantics=("parallel", "arbitrary"), vmem_limit_bytes=VMEM_LIMIT),
        name="qkv_rglru",
    )(x, g, wqT, wk, wvT, w_xg, conv_w, conv_b, w_ri, b_r, b_i, lru_lambda)


def _sublane_allmax(v):
    for sh in (4, 2, 1):
        v = jnp.maximum(v, pltpu.roll(v, sh, 0))
    return v


def _attn_kernel(lamp_ref, qT_ref, qTn_ref, k_ref, vT_ref, kext_ref, qext_ref, mask_ref, slope_ref, gsub_ref,
                 o_ref, qz_sc, s_sc, mx_sc, m_sc, acc_sc):
    g_heads = qT_ref.shape[1]
    tile = qT_ref.shape[3]
    i = pl.program_id(2)
    chains = [(g, c) for g in range(g_heads) for c in range(2)]
    nch = len(chains)
    ones_rows = jnp.ones((ACC_ROWS - V_DIM, tile), BF16)
    cur = jnp.bitwise_and(i, 1)

    def build_qz(src_ref, slot):
        row = lax.broadcasted_iota(jnp.int32, (V_DIM, tile), 0)
        for g in range(g_heads):
            q_both = src_ref[0, g].astype(F32)
            qz_sc[slot, g, 0, 0:V_DIM] = jnp.where(row < HEAD_DIM, q_both, 0.0).astype(BF16)
            qz_sc[slot, g, 1, 0:V_DIM] = jnp.where(row >= HEAD_DIM, q_both, 0.0).astype(BF16)
            qz_sc[slot, g, 0, V_DIM:2 * V_DIM] = qext_ref[g]
            qz_sc[slot, g, 1, V_DIM:2 * V_DIM] = qext_ref[g]

    def scores(slot, j, idx):
        g, c = chains[idx]
        kt = k_ref[0, pl.ds(pl.multiple_of(j * tile, tile), tile), g * V_DIM:(g + 1) * V_DIM]
        lhs = jnp.concatenate([kt, kext_ref[...]], axis=1)
        return jnp.dot(lhs, qz_sc[slot, g, c], preferred_element_type=F32)

    @pl.when(i == 0)
    def _():
        build_qz(qT_ref, 0)
        for idx in range(nch):
            first = scores(0, 0, idx)
            s_sc[idx] = first
            mx_sc[idx] = jnp.max(first.reshape(tile // 8, 8, tile), axis=0)

    m_sc[...] = jnp.full(m_sc.shape, NEG_BIG, F32)
    acc_sc[...] = jnp.zeros(acc_sc.shape, F32)

    def update_acc(idx, alpha, pv):
        acc = acc_sc[idx].reshape(ACC_ROWS // 8, 8, tile) * alpha[None] + pv.reshape(ACC_ROWS // 8, 8, tile)
        acc_sc[idx] = acc.reshape(ACC_ROWS, tile)

    def step(j, diagonal):
        dist = jnp.full((8, tile), i - j, jnp.int32).astype(F32)
        pending = None
        for idx, (g, c) in enumerate(chains):
            s = s_sc[idx]
            s_next = scores(1 - cur, 0, idx) if diagonal else scores(cur, j + 1, idx)
            dvec = slope_ref[g] * dist
            if diagonal:
                s = s + mask_ref[...]
            t = s.reshape(tile // 8, 8, tile)
            tmax = jnp.max(t, axis=0) if diagonal else mx_sc[idx]
            m_old = m_sc[idx]
            m_new = jnp.maximum(m_old, _sublane_allmax(tmax) - dvec)
            p = jnp.exp2(t - (m_new + dvec)[None])
            alpha = jnp.exp2(m_old - m_new)
            m_sc[idx] = m_new
            lhs = jnp.concatenate([vT_ref[0, g, j], ones_rows], axis=0)
            pv = jnp.dot(lhs, p.reshape(tile, tile).astype(BF16), preferred_element_type=F32)
            s_sc[idx] = s_next
            mx_sc[idx] = jnp.max(s_next.reshape(tile // 8, 8, tile), axis=0)
            if pending is not None:
                update_acc(*pending)
            pending = (idx, alpha, pv)
        update_acc(*pending)

    def body(j, carry):
        step(j, False)
        return carry

    lax.fori_loop(0, i, body, 0)
    build_qz(qTn_ref, 1 - cur)
    step(i, True)

    lp = lamp_ref[...]
    lam = (jnp.exp(jnp.sum(lp[0:1] * lp[1:2], axis=-1, keepdims=True))
           - jnp.exp(jnp.sum(lp[2:3] * lp[3:4], axis=-1, keepdims=True)) + LAM_INIT)
    for g in range(g_heads):
        a1 = acc_sc[2 * g]
        a2 = acc_sc[2 * g + 1]
        o1 = a1[0:V_DIM] / a1[V_DIM:V_DIM + 1]
        o2 = a2[0:V_DIM] / a2[V_DIM:V_DIM + 1]
        a = o1 - lam * o2
        ms = jnp.mean(a * a, axis=0, keepdims=True)
        y = a * lax.rsqrt(ms + EPS) * gsub_ref[...] * (1.0 - LAM_INIT)
        o_ref[0, g] = y.astype(o_ref.dtype)


def _attention(lamp, qT, k, vT5, kext, qext, mask, slope, gsub):
    b, h, _, s = qT.shape
    g = ATT_HEADS_PER_STEP
    t = ATT_TILE
    nq = s // t
    return pl.pallas_call(
        _attn_kernel,
        grid=(b, h // g, nq),
        in_specs=[
            pl.BlockSpec((4, HEAD_DIM), lambda bi, hi, qi: (0, 0)),
            pl.BlockSpec((1, g, V_DIM, t), lambda bi, hi, qi: (bi, hi, 0, qi)),
            pl.BlockSpec((1, g, V_DIM, t), lambda bi, hi, qi: (bi, hi, 0, jnp.minimum(qi + 1, nq - 1))),
            pl.BlockSpec((1, s, g * V_DIM), lambda bi, hi, qi: (bi, 0, hi)),
            pl.BlockSpec((1, g, nq, V_DIM, t), lambda bi, hi, qi: (bi, hi, 0, 0, 0)),
            pl.BlockSpec((t, V_DIM), lambda bi, hi, qi: (0, 0)),
            pl.BlockSpec((g, V_DIM, t), lambda bi, hi, qi: (hi, 0, 0)),
            pl.BlockSpec((t, t), lambda bi, hi, qi: (0, 0)),
            pl.BlockSpec((g, 8, t), lambda bi, hi, qi: (hi, 0, 0)),
            pl.BlockSpec((V_DIM, 1), lambda bi, hi, qi: (0, 0)),
        ],
        out_specs=pl.BlockSpec((1, g, V_DIM, t), lambda bi, hi, qi: (bi, hi, 0, qi)),
        out_shape=jax.ShapeDtypeStruct((b, h, V_DIM, s), BF16),
        scratch_shapes=[
            pltpu.VMEM((2, g, 2, 2 * V_DIM, t), BF16),
            pltpu.VMEM((2 * g, t, t), F32),
            pltpu.VMEM((2 * g, 8, t), F32),
            pltpu.VMEM((2 * g, 8, t), F32),
            pltpu.VMEM((2 * g, ACC_ROWS, t), F32),
        ],
        compiler_params=pltpu.CompilerParams(
            dimension_semantics=("parallel", "parallel", "arbitrary"), vmem_limit_bytes=VMEM_LIMIT),
        name="attention",
    )(lamp, qT, qT, k, vT5, kext, qext, mask, slope, gsub)


def _merge_ffn_kernel(x_ref, attT_ref, rnn_ref, gx_ref, wg_ref, bg_ref, wa_ref, wr_ref, wo_ref,
                      gm_ref, w1_ref, w2_ref, gf_ref, o_ref):
    x = x_ref[0]
    h = _rms(x, gx_ref[...]).astype(BF16)
    y_att = lax.dot_general(attT_ref[0], wa_ref[...], (((0,), (0,)), ((), ())),
                            preferred_element_type=F32)
    y_rnn = jnp.dot(rnn_ref[0], wr_ref[...], preferred_element_type=F32)
    gts = jax.nn.sigmoid(jnp.dot(h, wg_ref[...], preferred_element_type=F32) + bg_ref[...])
    m = gts[:, :D_MODEL] * y_att + gts[:, D_MODEL:] * y_rnn
    x1 = x + jnp.dot(m.astype(BF16), wo_ref[...], preferred_element_type=F32)
    h2 = _rms(x1, gm_ref[...]).astype(BF16)
    f = jnp.dot(h2, w1_ref[...], preferred_element_type=F32)
    f = jnp.square(jnp.maximum(f, 0.0)).astype(BF16)
    x2 = x1 + jnp.dot(f, w2_ref[...], preferred_element_type=F32)
    o_ref[0] = _rms(x2, gf_ref[...])


def _merge_ffn(x, attT, rnn, g_mix, w_gate, b_gate, w_att, w_rnn, w_o, g_mlp, w1, w2, g_final):
    b, s, d = x.shape
    tm = FFN_TM
    const = lambda shape: pl.BlockSpec(shape, lambda bi, si: (0,) * len(shape),
                                       pipeline_mode=pl.Buffered(1))
    return pl.pallas_call(
        _merge_ffn_kernel,
        grid=(b, s // tm),
        in_specs=[
            pl.BlockSpec((1, tm, d), lambda bi, si: (bi, si, 0)),
            pl.BlockSpec((1, D_V, tm), lambda bi, si: (bi, 0, si)),
            pl.BlockSpec((1, tm, D_RNN), lambda bi, si: (bi, si, 0)),
            const((1, d)), const((d, 2 * d)),
            const((1, 2 * d)),
            const((D_V, d)), const((D_RNN, d)), const((d, d)),
            const((1, d)),
            const((d, D_FF)), const((D_FF, d)),
            const((1, d)),
        ],
        out_specs=pl.BlockSpec((1, tm, d), lambda bi, si: (bi, si, 0)),
        out_shape=jax.ShapeDtypeStruct((b, s, d), F32),
        compiler_params=pltpu.CompilerParams(
            dimension_semantics=("parallel", "parallel"), vmem_limit_bytes=VMEM_LIMIT),
        name="merge_ffn",
    )(x, attT, rnn, g_mix, w_gate, b_gate, w_att, w_rnn, w_o, g_mlp, w1, w2, g_final)


def _split_bf16(v, pieces=3):
    out, r = [], np.asarray(v, np.float64)
    for _ in range(pieces):
        piece = r.astype(np.float32).astype(BF16)
        out.append(piece)
        r = r - piece.astype(np.float64)
    return out


def _alibi_tables(tile):
    slopes = 2.0 ** (-8.0 * np.arange(1, N_HEADS + 1, dtype=np.float64) / N_HEADS) * LOG2E
    kr = np.arange(tile, dtype=np.float64)
    qr = np.arange(tile, dtype=np.float64)
    assert tile <= 256 * 256
    kext = np.zeros((tile, V_DIM), np.float32)
    kext[:, 0:3] = np.mod(kr, 256.0)[:, None]
    kext[:, 3:6] = np.floor(kr / 256.0)[:, None]
    kext[:, 6:9] = 1.0
    qext = np.zeros((N_HEADS, V_DIM, tile), BF16)
    for n, piece in enumerate(_split_bf16(slopes)):
        qext[:, n, :] = piece[:, None]
        qext[:, 3 + n, :] = (piece.astype(np.float32) * 256.0).astype(BF16)[:, None]
    for n, piece in enumerate(_split_bf16(-slopes[:, None] * qr[None, :])):
        qext[:, 6 + n, :] = piece
    mask = np.where(kr[:, None] > qr[None, :], NEG_BIG, 0.0).astype(np.float32)
    slope_rows = np.broadcast_to((slopes * tile).astype(np.float32)[:, None, None], (N_HEADS, 8, tile))
    return jnp.asarray(kext, BF16), jnp.asarray(qext), jnp.asarray(mask), jnp.asarray(slope_rows)


def kernel(x, w_in, b_gate, g_mix, lambda_q1, lambda_k1, lambda_q2, lambda_k2, subln_g,
           conv_w, conv_b, w_r, b_r, w_i, b_i, lru_lambda, w_att_out, w_rnn_out, w_o,
           g_mlp, w_ff1, w_ff2, g_final):
    b, s, d = x.shape
    assert (d, w_in.shape[0]) == (D_MODEL, 1) and s % ATT_TILE == 0
    t = ATT_TILE
    w_in_b = w_in[0].astype(BF16)
    w_q, w_k, w_v = (w_in_b[:, n * D_QK:(n + 1) * D_QK] for n in range(3))
    w_ri = jnp.concatenate([w_r[0], w_i[0]], axis=-1).astype(BF16)
    qT, k, vT5, rnn = _qkv_rglru(x, g_mix, w_q.T, w_k, w_v.T, w_in_b[:, D_QKV:D_QKV + 2 * D_RNN],
                                 conv_w[0], conv_b, w_ri, b_r, b_i, lru_lambda)
    kext, qext, mask, slope_rows = _alibi_tables(t)
    lamp = jnp.concatenate([lambda_q1, lambda_k1, lambda_q2, lambda_k2], axis=0).astype(F32)
    attT = _attention(lamp, qT, k, vT5, kext, qext, mask, slope_rows, subln_g.reshape(V_DIM, 1))
    attT = attT.reshape(b, D_V, s)

    return _merge_ffn(x, attT, rnn, g_mix, w_in_b[:, D_QKV + 2 * D_RNN:], b_gate,
                      w_att_out[0].astype(BF16), w_rnn_out[0].astype(BF16), w_o[0].astype(BF16),
                      g_mlp, w_ff1[0].astype(BF16), w_ff2[0].astype(BF16), g_final.reshape(1, d))
```

```python
import math

import jax
import jax.numpy as jnp
import numpy as np
from jax import lax
from jax.experimental import pallas as pl
from jax.experimental.pallas import tpu as pltpu

D_MODEL = 1024
N_HEADS = 8
HEAD_DIM = 64
V_DIM = 2 * HEAD_DIM
ACC_ROWS = V_DIM + 16
D_QK = N_HEADS * 2 * HEAD_DIM
D_V = N_HEADS * V_DIM
D_RNN = D_MODEL
N_RNN_BLOCKS = 8
RNN_BLOCK = D_RNN // N_RNN_BLOCKS
CONV_WIDTH = 4
LRU_C = 8.0
D_FF = 4 * D_MODEL
D_QKV = 2 * D_QK + D_V
EPS = 1e-6
LAM_INIT = 0.8 - 0.6 * math.exp(-0.3 * 0)
LOG2E = 1.4426950408889634
NEG_BIG = -1e30
TINY = 1e-30

ATT_TILE = 512
ATT_HEADS_PER_STEP = 4
FFN_TM = 512
VMEM_LIMIT = 56 * 1024 * 1024

F32 = jnp.float32
BF16 = jnp.bfloat16


def _rms(x, g):
    return x * lax.rsqrt(jnp.mean(x * x, axis=-1, keepdims=True) + EPS) * g


def _zero_row(v):
    zero = jnp.minimum(jnp.abs(v[0:1, 0:RNN_BLOCK]), 0.0)
    return jnp.concatenate([zero] * (D_RNN // RNN_BLOCK), axis=1)


def _qkv_rglru_kernel(x_ref, g_ref, wqT_ref, wk_ref, wvT_ref, wxg_ref, cw_ref, cb_ref, wri_ref, br_ref, bi_ref,
                      lam_ref, qT_ref, k_ref, vT_ref, o_ref, xg_sc, o_sc, ch_sc, al_sc, ul_sc, c_sc):
    ts = x_ref.shape[1]
    half = ts // 2
    nr = half // 8
    nb = D_RNN // RNN_BLOCK
    sblk = pl.program_id(1)
    nt = (((1,), (1,)), ((), ()))

    @pl.when(sblk == 0)
    def _():
        xg_sc[0:nb, 0:8, :] = jnp.zeros((nb, 8, RNN_BLOCK), F32)
        ch_sc[...] = jnp.zeros(ch_sc.shape, F32)

    h = _rms(x_ref[0], g_ref[...]).astype(BF16)
    for base in (0, half):
        xg = jnp.dot(h[base:base + half], wxg_ref[...], preferred_element_type=F32)
        for c in range(2 * nb):
            xg_sc[c, 8 + base:8 + base + half, :] = xg[:, c * RNN_BLOCK:(c + 1) * RNN_BLOCK]

    nl = -lam_ref[...]
    sp = jnp.maximum(nl, 0.0) + jnp.log(1.0 + jnp.exp(-jnp.abs(nl)))
    cw = cw_ref[...]
    row = lax.broadcasted_iota(jnp.int32, (half, D_RNN), 0)

    def phase(col0, base, s):
        return jnp.concatenate([xg_sc[col0 + c, pl.ds(8 + base + s, nr, stride=8), :] for c in range(nb)], axis=1)

    def conv_gates(base):
        xph = {s: phase(0, base, s) for s in range(1 - CONV_WIDTH, 8)}
        xc = jnp.concatenate(
            [cb_ref[...] + sum(cw[CONV_WIDTH - 1 - d:CONV_WIDTH - d] * xph[s - d] for d in range(CONV_WIDTH))
             for s in range(8)], axis=0)
        xcb = xc.astype(BF16)
        r_parts, i_parts = [], []
        for n in range(N_RNN_BLOCKS):
            ri = jnp.dot(xcb[:, n * RNN_BLOCK:(n + 1) * RNN_BLOCK], wri_ref[n], preferred_element_type=F32)
            r_parts.append(ri[:, :RNN_BLOCK])
            i_parts.append(ri[:, RNN_BLOCK:])
        return xc, jnp.concatenate(r_parts, axis=1), jnp.concatenate(i_parts, axis=1)

    def recurrence(base, xc, r_pre, i_pre, c, first, z_gate=0.0, z_scan=0.0, z_out=0.0):
        r = jax.nn.sigmoid(r_pre + (br_ref[...] + z_gate))
        ig = jax.nn.sigmoid(i_pre + bi_ref[...])
        a = jnp.exp2((-LRU_C * LOG2E) * sp * r)
        y = 1.0 - a * a
        mult = y * lax.rsqrt(jnp.maximum(y, TINY))
        if first:
            mult = jnp.where(jnp.logical_and(row == 0, sblk == 0), 1.0, mult)
        u = mult * (ig * xc)
        hloc, ploc = [u[0:nr] + z_scan], [a[0:nr]]
        for s in range(1, 8):
            a_s = a[s * nr:(s + 1) * nr]
            hloc.append(a_s * hloc[-1] + u[s * nr:(s + 1) * nr])
            ploc.append(a_s * ploc[-1])
        ul_sc[...] = hloc[7]
        al_sc[...] = ploc[7]
        for rr in range(nr):
            c_sc[rr:rr + 1, :] = c
            c = ul_sc[rr:rr + 1, :] + al_sc[rr:rr + 1, :] * c
        c_in = c_sc[...]
        for s in range(8):
            out = (hloc[s] + ploc[s] * c_in) * jax.nn.gelu(phase(nb, base, s))
            if s == 7:
                out = out + z_out
            for cb in range(nb):
                o_sc[cb, pl.ds(base + s, nr, stride=8), :] = out[:, cb * RNN_BLOCK:(cb + 1) * RNN_BLOCK]
        return c

    xc_a, r_a, i_a = conv_gates(0)
    qT = lax.dot_general(wqT_ref[...], h, nt, preferred_element_type=F32)
    qT_ref[0] = (qT * (HEAD_DIM ** -0.5 * LOG2E)).astype(BF16).reshape(N_HEADS, V_DIM, ts)
    c_mid = recurrence(0, xc_a, r_a, i_a, ch_sc[...], True)

    xc_b, r_b, i_b = conv_gates(half)
    kk = jnp.dot(h, wk_ref[...], preferred_element_type=F32)
    k_ref[0] = kk.astype(BF16)
    vT = lax.dot_general(wvT_ref[...], h, nt, preferred_element_type=F32)
    vT_ref[0, :, 0] = vT.astype(BF16).reshape(N_HEADS, V_DIM, ts)
    ch_sc[...] = recurrence(half, xc_b, r_b, i_b, c_mid, False,
                            z_gate=_zero_row(qT), z_scan=_zero_row(kk), z_out=_zero_row(vT))

    for c in range(nb):
        xg_sc[c, 0:8, :] = xg_sc[c, ts:ts + 8, :]
    o_ref[0] = jnp.concatenate([o_sc[c] for c in range(nb)], axis=1).astype(o_ref.dtype)


def _qkv_rglru(x, g, wqT, wk, wvT, w_xg, conv_w, conv_b, w_ri, b_r, b_i, lru_lambda):
    b, s, d = x.shape
    ts = ATT_TILE
    nb = D_RNN // RNN_BLOCK
    const = lambda shape: pl.BlockSpec(shape, lambda bi, si: (0,) * len(shape))
    return pl.pallas_call(
        _qkv_rglru_kernel,
        grid=(b, s // ts),
        in_specs=[
            pl.BlockSpec((1, ts, d), lambda bi, si: (bi, si, 0)),
            const((1, d)), const((D_QK, d)), const((d, D_QK)), const((D_V, d)),
            const((d, 2 * D_RNN)), const((CONV_WIDTH, D_RNN)), const((1, D_RNN)),
            const((N_RNN_BLOCKS, RNN_BLOCK, 2 * RNN_BLOCK)),
            const((1, D_RNN)), const((1, D_RNN)), const((1, D_RNN)),
        ],
        out_specs=[
            pl.BlockSpec((1, N_HEADS, V_DIM, ts), lambda bi, si: (bi, 0, 0, si)),
            pl.BlockSpec((1, ts, D_QK), lambda bi, si: (bi, si, 0)),
            pl.BlockSpec((1, N_HEADS, 1, V_DIM, ts), lambda bi, si: (bi, 0, si, 0, 0)),
            pl.BlockSpec((1, ts, D_RNN), lambda bi, si: (bi, si, 0)),
        ],
        out_shape=[
            jax.ShapeDtypeStruct((b, N_HEADS, V_DIM, s), BF16),
            jax.ShapeDtypeStruct((b, s, D_QK), BF16),
            jax.ShapeDtypeStruct((b, N_HEADS, s // ts, V_DIM, ts), BF16),
            jax.ShapeDtypeStruct((b, s, D_RNN), BF16),
        ],
        scratch_shapes=[
            pltpu.VMEM((2 * nb, ts + 8, RNN_BLOCK), F32),
            pltpu.VMEM((nb, ts, RNN_BLOCK), F32),
            pltpu.VMEM((1, D_RNN), F32),
            pltpu.VMEM((ts // 16, D_RNN), F32),
            pltpu.VMEM((ts // 16, D_RNN), F32),
            pltpu.VMEM((ts // 16, D_RNN), F32),
        ],
        compiler_params=pltpu.CompilerParams(
            dimension_semantics=("parallel", "arbitrary"), vmem_limit_bytes=VMEM_LIMIT),
        name="qkv_rglru",
    )(x, g, wqT, wk, wvT, w_xg, conv_w, conv_b, w_ri, b_r, b_i, lru_lambda)


def _sublane_allmax(v):
    for sh in (4, 2, 1):
        v = jnp.maximum(v, pltpu.roll(v, sh, 0))
    return v


def _attn_kernel(lamp_ref, qT_ref, qTn_ref, k_ref, vT_ref, kext_ref, qext_ref, mask_ref, slope_ref, gsub_ref,
                 o_ref, qz_sc, s_sc, m_sc, acc_sc):
    g_heads = qT_ref.shape[1]
    tile = qT_ref.shape[3]
    i = pl.program_id(2)
    chains = [(g, c) for g in range(g_heads) for c in range(2)]
    nch = len(chains)
    ones_rows = jnp.ones((ACC_ROWS - V_DIM, tile), BF16)
    cur = jnp.bitwise_and(i, 1)

    def build_qz(src_ref, slot):
        row = lax.broadcasted_iota(jnp.int32, (V_DIM, tile), 0)
        for g in range(g_heads):
            q_both = src_ref[0, g].astype(F32)
            qz_sc[slot, g, 0, 0:V_DIM] = jnp.where(row < HEAD_DIM, q_both, 0.0).astype(BF16)
            qz_sc[slot, g, 1, 0:V_DIM] = jnp.where(row >= HEAD_DIM, q_both, 0.0).astype(BF16)
            qz_sc[slot, g, 0, V_DIM:2 * V_DIM] = qext_ref[g]
            qz_sc[slot, g, 1, V_DIM:2 * V_DIM] = qext_ref[g]

    def scores(slot, j, idx):
        g, c = chains[idx]
        kt = k_ref[0, pl.ds(pl.multiple_of(j * tile, tile), tile), g * V_DIM:(g + 1) * V_DIM]
        lhs = jnp.concatenate([kt, kext_ref[...]], axis=1)
        return jnp.dot(lhs, qz_sc[slot, g, c], preferred_element_type=F32)

    @pl.when(i == 0)
    def _():
        build_qz(qT_ref, 0)
        for idx in range(nch):
            s_sc[idx] = scores(0, 0, idx)

    m_sc[...] = jnp.full(m_sc.shape, NEG_BIG, F32)
    acc_sc[...] = jnp.zeros(acc_sc.shape, F32)

    def update_acc(idx, alpha, pv):
        acc = acc_sc[idx].reshape(ACC_ROWS // 8, 8, tile) * alpha[None] + pv.reshape(ACC_ROWS // 8, 8, tile)
        acc_sc[idx] = acc.reshape(ACC_ROWS, tile)

    def softmax_full(s, m_old, dvec):
        t = s.reshape(tile // 8, 8, tile)
        m_new = jnp.maximum(m_old, _sublane_allmax(jnp.max(t, axis=0)) - dvec)
        return m_new, jnp.exp2(t - (m_new + dvec)[None]).reshape(tile, tile)

    def softmax_diag(s, m_old, dvec):
        hh = tile // 2
        tri = mask_ref[...].reshape(hh // 8, 8, hh)
        t00 = s[0:hh, 0:hh].reshape(hh // 8, 8, hh) + tri
        t01 = s[0:hh, hh:].reshape(hh // 8, 8, hh)
        t11 = s[hh:, hh:].reshape(hh // 8, 8, hh) + tri
        tmax = jnp.concatenate([jnp.max(t00, axis=0),
                                jnp.maximum(jnp.max(t01, axis=0), jnp.max(t11, axis=0))], axis=1)
        m_new = jnp.maximum(m_old, _sublane_allmax(tmax) - dvec)
        sub = m_new + dvec
        p00 = jnp.exp2(t00 - sub[None, :, 0:hh]).reshape(hh, hh)
        p01 = jnp.exp2(t01 - sub[None, :, hh:]).reshape(hh, hh)
        p11 = jnp.exp2(t11 - sub[None, :, hh:]).reshape(hh, hh)
        p = jnp.concatenate([jnp.concatenate([p00, p01], axis=1),
                             jnp.concatenate([jnp.zeros((hh, hh), F32), p11], axis=1)], axis=0)
        return m_new, p

    def step(j, diagonal):
        dist = jnp.full((8, tile), i - j, jnp.int32).astype(F32)
        pending = None
        for idx, (g, c) in enumerate(chains):
            s = s_sc[idx]
            s_next = scores(1 - cur, 0, idx) if diagonal else scores(cur, j + 1, idx)
            dvec = slope_ref[g] * dist
            m_old = m_sc[idx]
            m_new, p = (softmax_diag if diagonal else softmax_full)(s, m_old, dvec)
            alpha = jnp.exp2(m_old - m_new)
            m_sc[idx] = m_new
            lhs = jnp.concatenate([vT_ref[0, g, j], ones_rows], axis=0)
            pv = jnp.dot(lhs, p.astype(BF16), preferred_element_type=F32)
            s_sc[idx] = s_next
            if pending is not None:
                update_acc(*pending)
            pending = (idx, alpha, pv)
        update_acc(*pending)

    def body(j, carry):
        step(j, False)
        return carry

    lax.fori_loop(0, i, body, 0)
    build_qz(qTn_ref, 1 - cur)
    step(i, True)

    lp = lamp_ref[...]
    lam = (jnp.exp(jnp.sum(lp[0:1] * lp[1:2], axis=-1, keepdims=True))
           - jnp.exp(jnp.sum(lp[2:3] * lp[3:4], axis=-1, keepdims=True)) + LAM_INIT)
    for g in range(g_heads):
        a1 = acc_sc[2 * g]
        a2 = acc_sc[2 * g + 1]
        o1 = a1[0:V_DIM] / a1[V_DIM:V_DIM + 1]
        o2 = a2[0:V_DIM] / a2[V_DIM:V_DIM + 1]
        a = o1 - lam * o2
        ms = jnp.mean(a * a, axis=0, keepdims=True)
        y = a * lax.rsqrt(ms + EPS) * gsub_ref[...] * (1.0 - LAM_INIT)
        o_ref[0, g] = y.astype(o_ref.dtype)


def _attention(lamp, qT, k, vT5, kext, qext, mask, slope, gsub):
    b, h, _, s = qT.shape
    g = ATT_HEADS_PER_STEP
    t = ATT_TILE
    nq = s // t
    return pl.pallas_call(
        _attn_kernel,
        grid=(b, h // g, nq),
        in_specs=[
            pl.BlockSpec((4, HEAD_DIM), lambda bi, hi, qi: (0, 0)),
            pl.BlockSpec((1, g, V_DIM, t), lambda bi, hi, qi: (bi, hi, 0, qi)),
            pl.BlockSpec((1, g, V_DIM, t), lambda bi, hi, qi: (bi, hi, 0, jnp.minimum(qi + 1, nq - 1))),
            pl.BlockSpec((1, s, g * V_DIM), lambda bi, hi, qi: (bi, 0, hi)),
            pl.BlockSpec((1, g, nq, V_DIM, t), lambda bi, hi, qi: (bi, hi, 0, 0, 0)),
            pl.BlockSpec((t, V_DIM), lambda bi, hi, qi: (0, 0)),
            pl.BlockSpec((g, V_DIM, t), lambda bi, hi, qi: (hi, 0, 0)),
            pl.BlockSpec((t // 2, t // 2), lambda bi, hi, qi: (0, 0)),
            pl.BlockSpec((g, 8, t), lambda bi, hi, qi: (hi, 0, 0)),
            pl.BlockSpec((V_DIM, 1), lambda bi, hi, qi: (0, 0)),
        ],
        out_specs=pl.BlockSpec((1, g, V_DIM, t), lambda bi, hi, qi: (bi, hi, 0, qi)),
        out_shape=jax.ShapeDtypeStruct((b, h, V_DIM, s), BF16),
        scratch_shapes=[
            pltpu.VMEM((2, g, 2, 2 * V_DIM, t), BF16),
            pltpu.VMEM((2 * g, t, t), F32),
            pltpu.VMEM((2 * g, 8, t), F32),
            pltpu.VMEM((2 * g, ACC_ROWS, t), F32),
        ],
        compiler_params=pltpu.CompilerParams(
            dimension_semantics=("parallel", "parallel", "arbitrary"), vmem_limit_bytes=VMEM_LIMIT),
        name="attention",
    )(lamp, qT, qT, k, vT5, kext, qext, mask, slope, gsub)


def _merge_ffn_kernel(x_ref, attT_ref, rnn_ref, gx_ref, wg_ref, bg_ref, wa_ref, wr_ref, wo_ref,
                      gm_ref, w1_ref, w2_ref, gf_ref, o_ref):
    x = x_ref[0]
    h = _rms(x, gx_ref[...]).astype(BF16)
    y_att = lax.dot_general(attT_ref[0], wa_ref[...], (((0,), (0,)), ((), ())),
                            preferred_element_type=F32)
    y_rnn = jnp.dot(rnn_ref[0], wr_ref[...], preferred_element_type=F32)
    gts = jax.nn.sigmoid(jnp.dot(h, wg_ref[...], preferred_element_type=F32) + bg_ref[...])
    m = gts[:, :D_MODEL] * y_att + gts[:, D_MODEL:] * y_rnn
    x1 = x + jnp.dot(m.astype(BF16), wo_ref[...], preferred_element_type=F32)
    h2 = _rms(x1, gm_ref[...]).astype(BF16)
    f = jnp.dot(h2, w1_ref[...], preferred_element_type=F32)
    f = jnp.square(jnp.maximum(f, 0.0)).astype(BF16)
    x2 = x1 + jnp.dot(f, w2_ref[...], preferred_element_type=F32)
    o_ref[0] = _rms(x2, gf_ref[...])


def _merge_ffn(x, attT, rnn, g_mix, w_gate, b_gate, w_att, w_rnn, w_o, g_mlp, w1, w2, g_final):
    b, s, d = x.shape
    tm = FFN_TM
    const = lambda shape: pl.BlockSpec(shape, lambda bi, si: (0,) * len(shape),
                                       pipeline_mode=pl.Buffered(1))
    return pl.pallas_call(
        _merge_ffn_kernel,
        grid=(b, s // tm),
        in_specs=[
            pl.BlockSpec((1, tm, d), lambda bi, si: (bi, si, 0)),
            pl.BlockSpec((1, D_V, tm), lambda bi, si: (bi, 0, si)),
            pl.BlockSpec((1, tm, D_RNN), lambda bi, si: (bi, si, 0)),
            const((1, d)), const((d, 2 * d)),
            const((1, 2 * d)),
            const((D_V, d)), const((D_RNN, d)), const((d, d)),
            const((1, d)),
            const((d, D_FF)), const((D_FF, d)),
            const((1, d)),
        ],
        out_specs=pl.BlockSpec((1, tm, d), lambda bi, si: (bi, si, 0)),
        out_shape=jax.ShapeDtypeStruct((b, s, d), F32),
        compiler_params=pltpu.CompilerParams(
            dimension_semantics=("parallel", "parallel"), vmem_limit_bytes=VMEM_LIMIT),
        name="merge_ffn",
    )(x, attT, rnn, g_mix, w_gate, b_gate, w_att, w_rnn, w_o, g_mlp, w1, w2, g_final)


def _split_bf16(v, pieces=3):
    out, r = [], np.asarray(v, np.float64)
    for _ in range(pieces):
        piece = r.astype(np.float32).astype(BF16)
        out.append(piece)
        r = r - piece.astype(np.float64)
    return out


def _alibi_tables(tile):
    slopes = 2.0 ** (-8.0 * np.arange(1, N_HEADS + 1, dtype=np.float64) / N_HEADS) * LOG2E
    kr = np.arange(tile, dtype=np.float64)
    qr = np.arange(tile, dtype=np.float64)
    assert tile <= 256 * 256
    kext = np.zeros((tile, V_DIM), np.float32)
    kext[:, 0:3] = np.mod(kr, 256.0)[:, None]
    kext[:, 3:6] = np.floor(kr / 256.0)[:, None]
    kext[:, 6:9] = 1.0
    qext = np.zeros((N_HEADS, V_DIM, tile), BF16)
    for n, piece in enumerate(_split_bf16(slopes)):
        qext[:, n, :] = piece[:, None]
        qext[:, 3 + n, :] = (piece.astype(np.float32) * 256.0).astype(BF16)[:, None]
    for n, piece in enumerate(_split_bf16(-slopes[:, None] * qr[None, :])):
        qext[:, 6 + n, :] = piece
    hh = tile // 2
    mask = np.where(kr[:hh, None] > qr[None, :hh], NEG_BIG, 0.0).astype(np.float32)
    slope_rows = np.broadcast_to((slopes * tile).astype(np.float32)[:, None, None], (N_HEADS, 8, tile))
    return jnp.asarray(kext, BF16), jnp.asarray(qext), jnp.asarray(mask), jnp.asarray(slope_rows)


def kernel(x, w_in, b_gate, g_mix, lambda_q1, lambda_k1, lambda_q2, lambda_k2, subln_g,
           conv_w, conv_b, w_r, b_r, w_i, b_i, lru_lambda, w_att_out, w_rnn_out, w_o,
           g_mlp, w_ff1, w_ff2, g_final):
    b, s, d = x.shape
    assert (d, w_in.shape[0]) == (D_MODEL, 1) and s % ATT_TILE == 0
    t = ATT_TILE
    w_in_b = w_in[0].astype(BF16)
    w_q, w_k, w_v = (w_in_b[:, n * D_QK:(n + 1) * D_QK] for n in range(3))
    w_ri = jnp.concatenate([w_r[0], w_i[0]], axis=-1).astype(BF16)
    qT, k, vT5, rnn = _qkv_rglru(x, g_mix, w_q.T, w_k, w_v.T, w_in_b[:, D_QKV:D_QKV + 2 * D_RNN],
                                 conv_w[0], conv_b, w_ri, b_r, b_i, lru_lambda)
    kext, qext, mask, slope_rows = _alibi_tables(t)
    lamp = jnp.concatenate([lambda_q1, lambda_k1, lambda_q2, lambda_k2], axis=0).astype(F32)
    attT = _attention(lamp, qT, k, vT5, kext, qext, mask, slope_rows, subln_g.reshape(V_DIM, 1))
    attT = attT.reshape(b, D_V, s)

    return _merge_ffn(x, attT, rnn, g_mix, w_in_b[:, D_QKV + 2 * D_RNN:], b_gate,
                      w_att_out[0].astype(BF16), w_rnn_out[0].astype(BF16), w_o[0].astype(BF16),
                      g_mlp, w_ff1[0].astype(BF16), w_ff2[0].astype(BF16), g_final.reshape(1, d))
```

```python
import math

import jax
import jax.numpy as jnp
import numpy as np
from jax import lax
from jax.experimental import pallas as pl
from jax.experimental.pallas import tpu as pltpu

D_MODEL = 1024
N_HEADS = 8
HEAD_DIM = 64
V_DIM = 2 * HEAD_DIM
ACC_ROWS = V_DIM + 16
D_QK = N_HEADS * 2 * HEAD_DIM
D_V = N_HEADS * V_DIM
D_RNN = D_MODEL
N_RNN_BLOCKS = 8
RNN_BLOCK = D_RNN // N_RNN_BLOCKS
CONV_WIDTH = 4
LRU_C = 8.0
D_FF = 4 * D_MODEL
D_QKV = 2 * D_QK + D_V
EPS = 1e-6
LAM_INIT = 0.8 - 0.6 * math.exp(-0.3 * 0)
LOG2E = 1.4426950408889634
NEG_BIG = -1e30
TINY = 1e-30

ATT_TILE = 512
ATT_HEADS_PER_STEP = 4
DIAG_BLOCKS = 4
FFN_TM = 512
VMEM_LIMIT = 56 * 1024 * 1024

F32 = jnp.float32
BF16 = jnp.bfloat16


def _rms(x, g):
    return x * lax.rsqrt(jnp.mean(x * x, axis=-1, keepdims=True) + EPS) * g


def _zero_row(v):
    zero = jnp.minimum(jnp.abs(v[0:1, 0:RNN_BLOCK]), 0.0)
    return jnp.concatenate([zero] * (D_RNN // RNN_BLOCK), axis=1)


def _qkv_rglru_kernel(x_ref, g_ref, wqT_ref, wk_ref, wvT_ref, wxg_ref, cw_ref, cb_ref, wri_ref, br_ref, bi_ref,
                      lam_ref, qT_ref, k_ref, vT_ref, o_ref, xg_sc, o_sc, ch_sc, al_sc, ul_sc, c_sc):
    ts = x_ref.shape[1]
    half = ts // 2
    nr = half // 8
    nb = D_RNN // RNN_BLOCK
    sblk = pl.program_id(1)
    nt = (((1,), (1,)), ((), ()))

    @pl.when(sblk == 0)
    def _():
        xg_sc[0:nb, 0:8, :] = jnp.zeros((nb, 8, RNN_BLOCK), F32)
        ch_sc[...] = jnp.zeros(ch_sc.shape, F32)

    h = _rms(x_ref[0], g_ref[...]).astype(BF16)
    for base in (0, half):
        xg = jnp.dot(h[base:base + half], wxg_ref[...], preferred_element_type=F32)
        for c in range(2 * nb):
            xg_sc[c, 8 + base:8 + base + half, :] = xg[:, c * RNN_BLOCK:(c + 1) * RNN_BLOCK]

    nl = -lam_ref[...]
    sp = jnp.maximum(nl, 0.0) + jnp.log(1.0 + jnp.exp(-jnp.abs(nl)))
    cw = cw_ref[...]
    row = lax.broadcasted_iota(jnp.int32, (half, D_RNN), 0)

    def phase(col0, base, s):
        return jnp.concatenate([xg_sc[col0 + c, pl.ds(8 + base + s, nr, stride=8), :] for c in range(nb)], axis=1)

    def conv_gates(base):
        xph = {s: phase(0, base, s) for s in range(1 - CONV_WIDTH, 8)}
        xc = jnp.concatenate(
            [cb_ref[...] + sum(cw[CONV_WIDTH - 1 - d:CONV_WIDTH - d] * xph[s - d] for d in range(CONV_WIDTH))
             for s in range(8)], axis=0)
        xcb = xc.astype(BF16)
        r_parts, i_parts = [], []
        for n in range(N_RNN_BLOCKS):
            ri = jnp.dot(xcb[:, n * RNN_BLOCK:(n + 1) * RNN_BLOCK], wri_ref[n], preferred_element_type=F32)
            r_parts.append(ri[:, :RNN_BLOCK])
            i_parts.append(ri[:, RNN_BLOCK:])
        return xc, jnp.concatenate(r_parts, axis=1), jnp.concatenate(i_parts, axis=1)

    def recurrence(base, xc, r_pre, i_pre, c, first, z_gate=0.0, z_scan=0.0, z_out=0.0):
        r = jax.nn.sigmoid(r_pre + (br_ref[...] + z_gate))
        ig = jax.nn.sigmoid(i_pre + bi_ref[...])
        a = jnp.exp2((-LRU_C * LOG2E) * sp * r)
        y = 1.0 - a * a
        mult = y * lax.rsqrt(jnp.maximum(y, TINY))
        if first:
            mult = jnp.where(jnp.logical_and(row == 0, sblk == 0), 1.0, mult)
        u = mult * (ig * xc)
        hloc, ploc = [u[0:nr] + z_scan], [a[0:nr]]
        for s in range(1, 8):
            a_s = a[s * nr:(s + 1) * nr]
            hloc.append(a_s * hloc[-1] + u[s * nr:(s + 1) * nr])
            ploc.append(a_s * ploc[-1])
        ul_sc[...] = hloc[7]
        al_sc[...] = ploc[7]
        for rr in range(nr):
            c_sc[rr:rr + 1, :] = c
            c = ul_sc[rr:rr + 1, :] + al_sc[rr:rr + 1, :] * c
        c_in = c_sc[...]
        for s in range(8):
            out = (hloc[s] + ploc[s] * c_in) * jax.nn.gelu(phase(nb, base, s))
            if s == 7:
                out = out + z_out
            for cb in range(nb):
                o_sc[cb, pl.ds(base + s, nr, stride=8), :] = out[:, cb * RNN_BLOCK:(cb + 1) * RNN_BLOCK]
        return c

    xc_a, r_a, i_a = conv_gates(0)
    qT = lax.dot_general(wqT_ref[...], h, nt, preferred_element_type=F32)
    qT_ref[0] = (qT * (HEAD_DIM ** -0.5 * LOG2E)).astype(BF16).reshape(N_HEADS, V_DIM, ts)
    c_mid = recurrence(0, xc_a, r_a, i_a, ch_sc[...], True)

    xc_b, r_b, i_b = conv_gates(half)
    kk = jnp.dot(h, wk_ref[...], preferred_element_type=F32)
    k_ref[0] = kk.astype(BF16)
    vT = lax.dot_general(wvT_ref[...], h, nt, preferred_element_type=F32)
    vT_ref[0, :, 0] = vT.astype(BF16).reshape(N_HEADS, V_DIM, ts)
    ch_sc[...] = recurrence(half, xc_b, r_b, i_b, c_mid, False,
                            z_gate=_zero_row(qT), z_scan=_zero_row(kk), z_out=_zero_row(vT))

    for c in range(nb):
        xg_sc[c, 0:8, :] = xg_sc[c, ts:ts + 8, :]
    o_ref[0] = jnp.concatenate([o_sc[c] for c in range(nb)], axis=1).astype(o_ref.dtype)


def _qkv_rglru(x, g, wqT, wk, wvT, w_xg, conv_w, conv_b, w_ri, b_r, b_i, lru_lambda):
    b, s, d = x.shape
    ts = ATT_TILE
    nb = D_RNN // RNN_BLOCK
    const = lambda shape: pl.BlockSpec(shape, lambda bi, si: (0,) * len(shape))
    return pl.pallas_call(
        _qkv_rglru_kernel,
        grid=(b, s // ts),
        in_specs=[
            pl.BlockSpec((1, ts, d), lambda bi, si: (bi, si, 0)),
            const((1, d)), const((D_QK, d)), const((d, D_QK)), const((D_V, d)),
            const((d, 2 * D_RNN)), const((CONV_WIDTH, D_RNN)), const((1, D_RNN)),
            const((N_RNN_BLOCKS, RNN_BLOCK, 2 * RNN_BLOCK)),
            const((1, D_RNN)), const((1, D_RNN)), const((1, D_RNN)),
        ],
        out_specs=[
            pl.BlockSpec((1, N_HEADS, V_DIM, ts), lambda bi, si: (bi, 0, 0, si)),
            pl.BlockSpec((1, ts, D_QK), lambda bi, si: (bi, si, 0)),
            pl.BlockSpec((1, N_HEADS, 1, V_DIM, ts), lambda bi, si: (bi, 0, si, 0, 0)),
            pl.BlockSpec((1, ts, D_RNN), lambda bi, si: (bi, si, 0)),
        ],
        out_shape=[
            jax.ShapeDtypeStruct((b, N_HEADS, V_DIM, s), BF16),
            jax.ShapeDtypeStruct((b, s, D_QK), BF16),
            jax.ShapeDtypeStruct((b, N_HEADS, s // ts, V_DIM, ts), BF16),
            jax.ShapeDtypeStruct((b, s, D_RNN), BF16),
        ],
        scratch_shapes=[
            pltpu.VMEM((2 * nb, ts + 8, RNN_BLOCK), F32),
            pltpu.VMEM((nb, ts, RNN_BLOCK), F32),
            pltpu.VMEM((1, D_RNN), F32),
            pltpu.VMEM((ts // 16, D_RNN), F32),
            pltpu.VMEM((ts // 16, D_RNN), F32),
            pltpu.VMEM((ts // 16, D_RNN), F32),
        ],
        compiler_params=pltpu.CompilerParams(
            dimension_semantics=("parallel", "arbitrary"), vmem_limit_bytes=VMEM_LIMIT),
        name="qkv_rglru",
    )(x, g, wqT, wk, wvT, w_xg, conv_w, conv_b, w_ri, b_r, b_i, lru_lambda)


def _sublane_allmax(v):
    for sh in (4, 2, 1):
        v = jnp.maximum(v, pltpu.roll(v, sh, 0))
    return v


def _attn_kernel(lamp_ref, qT_ref, qTn_ref, k_ref, vT_ref, kext_ref, qext_ref, mask_ref, slope_ref, gsub_ref,
                 o_ref, qz_sc, s_sc, m_sc, acc_sc):
    g_heads = qT_ref.shape[1]
    tile = qT_ref.shape[3]
    i = pl.program_id(2)
    chains = [(g, c) for g in range(g_heads) for c in range(2)]
    nch = len(chains)
    ones_rows = jnp.ones((ACC_ROWS - V_DIM, tile), BF16)
    cur = jnp.bitwise_and(i, 1)

    def build_qz(src_ref, slot):
        row = lax.broadcasted_iota(jnp.int32, (V_DIM, tile), 0)
        for g in range(g_heads):
            q_both = src_ref[0, g].astype(F32)
            qz_sc[slot, g, 0, 0:V_DIM] = jnp.where(row < HEAD_DIM, q_both, 0.0).astype(BF16)
            qz_sc[slot, g, 1, 0:V_DIM] = jnp.where(row >= HEAD_DIM, q_both, 0.0).astype(BF16)
            qz_sc[slot, g, 0, V_DIM:2 * V_DIM] = qext_ref[g]
            qz_sc[slot, g, 1, V_DIM:2 * V_DIM] = qext_ref[g]

    def scores(slot, j, idx):
        g, c = chains[idx]
        kt = k_ref[0, pl.ds(pl.multiple_of(j * tile, tile), tile), g * V_DIM:(g + 1) * V_DIM]
        lhs = jnp.concatenate([kt, kext_ref[...]], axis=1)
        return jnp.dot(lhs, qz_sc[slot, g, c], preferred_element_type=F32)

    @pl.when(i == 0)
    def _():
        build_qz(qT_ref, 0)
        for idx in range(nch):
            s_sc[idx] = scores(0, 0, idx)

    m_sc[...] = jnp.full(m_sc.shape, NEG_BIG, F32)
    acc_sc[...] = jnp.zeros(acc_sc.shape, F32)

    def update_acc(idx, alpha, pv):
        acc = acc_sc[idx].reshape(ACC_ROWS // 8, 8, tile) * alpha[None] + pv.reshape(ACC_ROWS // 8, 8, tile)
        acc_sc[idx] = acc.reshape(ACC_ROWS, tile)

    def softmax_full(s, m_old, dvec):
        t = s.reshape(tile // 8, 8, tile)
        m_new = jnp.maximum(m_old, _sublane_allmax(jnp.max(t, axis=0)) - dvec)
        return m_new, jnp.exp2(t - (m_new + dvec)[None]).reshape(tile, tile)

    def softmax_diag(s, m_old, dvec):
        nblk = DIAG_BLOCKS
        bs = tile // nblk
        tri = mask_ref[...].reshape(bs // 8, 8, bs)

        def block(kb, qb):
            t = s[kb * bs:(kb + 1) * bs, qb * bs:(qb + 1) * bs].reshape(bs // 8, 8, bs)
            return t + tri if kb == qb else t

        t_blk = {(kb, qb): block(kb, qb) for qb in range(nblk) for kb in range(qb + 1)}
        col_max = []
        for qb in range(nblk):
            m = jnp.max(t_blk[(0, qb)], axis=0)
            for kb in range(1, qb + 1):
                m = jnp.maximum(m, jnp.max(t_blk[(kb, qb)], axis=0))
            col_max.append(m)
        m_new = jnp.maximum(m_old, _sublane_allmax(jnp.concatenate(col_max, axis=1)) - dvec)
        sub = m_new + dvec
        rows = []
        for kb in range(nblk):
            rows.append(jnp.concatenate(
                [jnp.exp2(t_blk[(kb, qb)] - sub[None, :, qb * bs:(qb + 1) * bs]).reshape(bs, bs) if kb <= qb
                 else jnp.zeros((bs, bs), F32) for qb in range(nblk)], axis=1))
        return m_new, jnp.concatenate(rows, axis=0)

    def step(j, diagonal):
        dist = jnp.full((8, tile), i - j, jnp.int32).astype(F32)
        pending = None
        for idx, (g, c) in enumerate(chains):
            s = s_sc[idx]
            s_next = scores(1 - cur, 0, idx) if diagonal else scores(cur, j + 1, idx)
            dvec = slope_ref[g] * dist
            m_old = m_sc[idx]
            m_new, p = (softmax_diag if diagonal else softmax_full)(s, m_old, dvec)
            alpha = jnp.exp2(m_old - m_new)
            m_sc[idx] = m_new
            lhs = jnp.concatenate([vT_ref[0, g, j], ones_rows], axis=0)
            pv = jnp.dot(lhs, p.astype(BF16), preferred_element_type=F32)
            s_sc[idx] = s_next
            if pending is not None:
                update_acc(*pending)
            pending = (idx, alpha, pv)
        update_acc(*pending)

    def body(j, carry):
        step(j, False)
        return carry

    lax.fori_loop(0, i, body, 0)
    build_qz(qTn_ref, 1 - cur)
    step(i, True)

    lp = lamp_ref[...]
    lam = (jnp.exp(jnp.sum(lp[0:1] * lp[1:2], axis=-1, keepdims=True))
           - jnp.exp(jnp.sum(lp[2:3] * lp[3:4], axis=-1, keepdims=True)) + LAM_INIT)
    for g in range(g_heads):
        a1 = acc_sc[2 * g]
        a2 = acc_sc[2 * g + 1]
        o1 = a1[0:V_DIM] / a1[V_DIM:V_DIM + 1]
        o2 = a2[0:V_DIM] / a2[V_DIM:V_DIM + 1]
        a = o1 - lam * o2
        ms = jnp.mean(a * a, axis=0, keepdims=True)
        y = a * lax.rsqrt(ms + EPS) * gsub_ref[...] * (1.0 - LAM_INIT)
        o_ref[0, g] = y.astype(o_ref.dtype)


def _attention(lamp, qT, k, vT5, kext, qext, mask, slope, gsub):
    b, h, _, s = qT.shape
    g = ATT_HEADS_PER_STEP
    t = ATT_TILE
    nq = s // t
    return pl.pallas_call(
        _attn_kernel,
        grid=(b, h // g, nq),
        in_specs=[
            pl.BlockSpec((4, HEAD_DIM), lambda bi, hi, qi: (0, 0)),
            pl.BlockSpec((1, g, V_DIM, t), lambda bi, hi, qi: (bi, hi, 0, qi)),
            pl.BlockSpec((1, g, V_DIM, t), lambda bi, hi, qi: (bi, hi, 0, jnp.minimum(qi + 1, nq - 1))),
            pl.BlockSpec((1, s, g * V_DIM), lambda bi, hi, qi: (bi, 0, hi)),
            pl.BlockSpec((1, g, nq, V_DIM, t), lambda bi, hi, qi: (bi, hi, 0, 0, 0)),
            pl.BlockSpec((t, V_DIM), lambda bi, hi, qi: (0, 0)),
            pl.BlockSpec((g, V_DIM, t), lambda bi, hi, qi: (hi, 0, 0)),
            pl.BlockSpec((t // DIAG_BLOCKS, t // DIAG_BLOCKS), lambda bi, hi, qi: (0, 0)),
            pl.BlockSpec((g, 8, t), lambda bi, hi, qi: (hi, 0, 0)),
            pl.BlockSpec((V_DIM, 1), lambda bi, hi, qi: (0, 0)),
        ],
        out_specs=pl.BlockSpec((1, g, V_DIM, t), lambda bi, hi, qi: (bi, hi, 0, qi)),
        out_shape=jax.ShapeDtypeStruct((b, h, V_DIM, s), BF16),
        scratch_shapes=[
            pltpu.VMEM((2, g, 2, 2 * V_DIM, t), BF16),
            pltpu.VMEM((2 * g, t, t), F32),
            pltpu.VMEM((2 * g, 8, t), F32),
            pltpu.VMEM((2 * g, ACC_ROWS, t), F32),
        ],
        compiler_params=pltpu.CompilerParams(
            dimension_semantics=("parallel", "parallel", "arbitrary"), vmem_limit_bytes=VMEM_LIMIT),
        name="attention",
    )(lamp, qT, qT, k, vT5, kext, qext, mask, slope, gsub)


def _merge_ffn_kernel(x_ref, attT_ref, rnn_ref, gx_ref, wg_ref, bg_ref, wa_ref, wr_ref, wo_ref,
                      gm_ref, w1_ref, w2_ref, gf_ref, o_ref):
    x = x_ref[0]
    h = _rms(x, gx_ref[...]).astype(BF16)
    y_att = lax.dot_general(attT_ref[0], wa_ref[...], (((0,), (0,)), ((), ())),
                            preferred_element_type=F32)
    y_rnn = jnp.dot(rnn_ref[0], wr_ref[...], preferred_element_type=F32)
    gts = jax.nn.sigmoid(jnp.dot(h, wg_ref[...], preferred_element_type=F32) + bg_ref[...])
    m = gts[:, :D_MODEL] * y_att + gts[:, D_MODEL:] * y_rnn
    x1 = x + jnp.dot(m.astype(BF16), wo_ref[...], preferred_element_type=F32)
    h2 = _rms(x1, gm_ref[...]).astype(BF16)
    f = jnp.dot(h2, w1_ref[...], preferred_element_type=F32)
    f = jnp.square(jnp.maximum(f, 0.0)).astype(BF16)
    x2 = x1 + jnp.dot(f, w2_ref[...], preferred_element_type=F32)
    o_ref[0] = _rms(x2, gf_ref[...])


def _merge_ffn(x, attT, rnn, g_mix, w_gate, b_gate, w_att, w_rnn, w_o, g_mlp, w1, w2, g_final):
    b, s, d = x.shape
    tm = FFN_TM
    const = lambda shape: pl.BlockSpec(shape, lambda bi, si: (0,) * len(shape),
                                       pipeline_mode=pl.Buffered(1))
    return pl.pallas_call(
        _merge_ffn_kernel,
        grid=(b, s // tm),
        in_specs=[
            pl.BlockSpec((1, tm, d), lambda bi, si: (bi, si, 0)),
            pl.BlockSpec((1, D_V, tm), lambda bi, si: (bi, 0, si)),
            pl.BlockSpec((1, tm, D_RNN), lambda bi, si: (bi, si, 0)),
            const((1, d)), const((d, 2 * d)),
            const((1, 2 * d)),
            const((D_V, d)), const((D_RNN, d)), const((d, d)),
            const((1, d)),
            const((d, D_FF)), const((D_FF, d)),
            const((1, d)),
        ],
        out_specs=pl.BlockSpec((1, tm, d), lambda bi, si: (bi, si, 0)),
        out_shape=jax.ShapeDtypeStruct((b, s, d), F32),
        compiler_params=pltpu.CompilerParams(
            dimension_semantics=("parallel", "parallel"), vmem_limit_bytes=VMEM_LIMIT),
        name="merge_ffn",
    )(x, attT, rnn, g_mix, w_gate, b_gate, w_att, w_rnn, w_o, g_mlp, w1, w2, g_final)


def _split_bf16(v, pieces=3):
    out, r = [], np.asarray(v, np.float64)
    for _ in range(pieces):
        piece = r.astype(np.float32).astype(BF16)
        out.append(piece)
        r = r - piece.astype(np.float64)
    return out


def _alibi_tables(tile):
    slopes = 2.0 ** (-8.0 * np.arange(1, N_HEADS + 1, dtype=np.float64) / N_HEADS) * LOG2E
    kr = np.arange(tile, dtype=np.float64)
    qr = np.arange(tile, dtype=np.float64)
    assert tile <= 256 * 256
    kext = np.zeros((tile, V_DIM), np.float32)
    kext[:, 0:3] = np.mod(kr, 256.0)[:, None]
    kext[:, 3:6] = np.floor(kr / 256.0)[:, None]
    kext[:, 6:9] = 1.0
    qext = np.zeros((N_HEADS, V_DIM, tile), BF16)
    for n, piece in enumerate(_split_bf16(slopes)):
        qext[:, n, :] = piece[:, None]
        qext[:, 3 + n, :] = (piece.astype(np.float32) * 256.0).astype(BF16)[:, None]
    for n, piece in enumerate(_split_bf16(-slopes[:, None] * qr[None, :])):
        qext[:, 6 + n, :] = piece
    hh = tile // DIAG_BLOCKS
    mask = np.where(kr[:hh, None] > qr[None, :hh], NEG_BIG, 0.0).astype(np.float32)
    slope_rows = np.broadcast_to((slopes * tile).astype(np.float32)[:, None, None], (N_HEADS, 8, tile))
    return jnp.asarray(kext, BF16), jnp.asarray(qext), jnp.asarray(mask), jnp.asarray(slope_rows)


def kernel(x, w_in, b_gate, g_mix, lambda_q1, lambda_k1, lambda_q2, lambda_k2, subln_g,
           conv_w, conv_b, w_r, b_r, w_i, b_i, lru_lambda, w_att_out, w_rnn_out, w_o,
           g_mlp, w_ff1, w_ff2, g_final):
    b, s, d = x.shape
    assert (d, w_in.shape[0]) == (D_MODEL, 1) and s % ATT_TILE == 0
    t = ATT_TILE
    w_in_b = w_in[0].astype(BF16)
    w_q, w_k, w_v = (w_in_b[:, n * D_QK:(n + 1) * D_QK] for n in range(3))
    w_ri = jnp.concatenate([w_r[0], w_i[0]], axis=-1).astype(BF16)
    qT, k, vT5, rnn = _qkv_rglru(x, g_mix, w_q.T, w_k, w_v.T, w_in_b[:, D_QKV:D_QKV + 2 * D_RNN],
                                 conv_w[0], conv_b, w_ri, b_r, b_i, lru_lambda)
    kext, qext, mask, slope_rows = _alibi_tables(t)
    lamp = jnp.concatenate([lambda_q1, lambda_k1, lambda_q2, lambda_k2], axis=0).astype(F32)
    attT = _attention(lamp, qT, k, vT5, kext, qext, mask, slope_rows, subln_g.reshape(V_DIM, 1))
    attT = attT.reshape(b, D_V, s)

    return _merge_ffn(x, attT, rnn, g_mix, w_in_b[:, D_QKV + 2 * D_RNN:], b_gate,
                      w_att_out[0].astype(BF16), w_rnn_out[0].astype(BF16), w_o[0].astype(BF16),
                      g_mlp, w_ff1[0].astype(BF16), w_ff2[0].astype(BF16), g_final.reshape(1, d))
```

```python
import math

import jax
import jax.numpy as jnp
import numpy as np
from jax import lax
from jax.experimental import pallas as pl
from jax.experimental.pallas import tpu as pltpu

D_MODEL = 1024
N_HEADS = 8
HEAD_DIM = 64
V_DIM = 2 * HEAD_DIM
ACC_ROWS = V_DIM + 16
D_QK = N_HEADS * 2 * HEAD_DIM
D_V = N_HEADS * V_DIM
D_RNN = D_MODEL
N_RNN_BLOCKS = 8
RNN_BLOCK = D_RNN // N_RNN_BLOCKS
CONV_WIDTH = 4
LRU_C = 8.0
D_FF = 4 * D_MODEL
D_QKV = 2 * D_QK + D_V
EPS = 1e-6
LAM_INIT = 0.8 - 0.6 * math.exp(-0.3 * 0)
LOG2E = 1.4426950408889634
NEG_BIG = -1e30
TINY = 1e-30

ATT_TILE = 512
ATT_HEADS_PER_STEP = 4
DIAG_BLOCKS = 4
FFN_TM = 512
VMEM_LIMIT = 56 * 1024 * 1024

F32 = jnp.float32
BF16 = jnp.bfloat16


def _rms(x, g):
    return x * lax.rsqrt(jnp.mean(x * x, axis=-1, keepdims=True) + EPS) * g


def _zero_row(v):
    zero = jnp.minimum(jnp.abs(v[0:1, 0:RNN_BLOCK]), 0.0)
    return jnp.concatenate([zero] * (D_RNN // RNN_BLOCK), axis=1)


def _qkv_rglru_kernel(x_ref, g_ref, wqT_ref, wk_ref, wvT_ref, wxg_ref, cw_ref, cb_ref, wri_ref, br_ref, bi_ref,
                      lam_ref, qT_ref, k_ref, vT_ref, o_ref, xg_sc, o_sc, ch_sc, al_sc, ul_sc, c_sc):
    ts = x_ref.shape[1]
    half = ts // 2
    nr = half // 8
    nb = D_RNN // RNN_BLOCK
    sblk = pl.program_id(1)
    nt = (((1,), (1,)), ((), ()))

    @pl.when(sblk == 0)
    def _():
        xg_sc[0:nb, 0:8, :] = jnp.zeros((nb, 8, RNN_BLOCK), F32)
        ch_sc[...] = jnp.zeros(ch_sc.shape, F32)

    h = _rms(x_ref[0], g_ref[...]).astype(BF16)
    for base in (0, half):
        xg = jnp.dot(h[base:base + half], wxg_ref[...], preferred_element_type=F32)
        for c in range(2 * nb):
            xg_sc[c, 8 + base:8 + base + half, :] = xg[:, c * RNN_BLOCK:(c + 1) * RNN_BLOCK]

    nl = -lam_ref[...]
    sp = jnp.maximum(nl, 0.0) + jnp.log(1.0 + jnp.exp(-jnp.abs(nl)))
    cw = cw_ref[...]
    row = lax.broadcasted_iota(jnp.int32, (half, D_RNN), 0)

    def phase(col0, base, s):
        return jnp.concatenate([xg_sc[col0 + c, pl.ds(8 + base + s, nr, stride=8), :] for c in range(nb)], axis=1)

    def conv_gates(base):
        xph = {s: phase(0, base, s) for s in range(1 - CONV_WIDTH, 8)}
        xc = jnp.concatenate(
            [cb_ref[...] + sum(cw[CONV_WIDTH - 1 - d:CONV_WIDTH - d] * xph[s - d] for d in range(CONV_WIDTH))
             for s in range(8)], axis=0)
        xcb = xc.astype(BF16)
        r_parts, i_parts = [], []
        for n in range(N_RNN_BLOCKS):
            ri = jnp.dot(xcb[:, n * RNN_BLOCK:(n + 1) * RNN_BLOCK], wri_ref[n], preferred_element_type=F32)
            r_parts.append(ri[:, :RNN_BLOCK])
            i_parts.append(ri[:, RNN_BLOCK:])
        return xc, jnp.concatenate(r_parts, axis=1), jnp.concatenate(i_parts, axis=1)

    def recurrence(base, xc, r_pre, i_pre, c, first, z_gate=0.0, z_scan=0.0, z_out=0.0):
        r = jax.nn.sigmoid(r_pre + (br_ref[...] + z_gate))
        ig = jax.nn.sigmoid(i_pre + bi_ref[...])
        a = jnp.exp2((-LRU_C * LOG2E) * sp * r)
        y = 1.0 - a * a
        mult = y * lax.rsqrt(jnp.maximum(y, TINY))
        if first:
            mult = jnp.where(jnp.logical_and(row == 0, sblk == 0), 1.0, mult)
        u = mult * (ig * xc)
        hloc, ploc = [u[0:nr] + z_scan], [a[0:nr]]
        for s in range(1, 8):
            a_s = a[s * nr:(s + 1) * nr]
            hloc.append(a_s * hloc[-1] + u[s * nr:(s + 1) * nr])
            ploc.append(a_s * ploc[-1])
        ul_sc[...] = hloc[7]
        al_sc[...] = ploc[7]
        for rr in range(nr):
            c_sc[rr:rr + 1, :] = c
            c = ul_sc[rr:rr + 1, :] + al_sc[rr:rr + 1, :] * c
        c_in = c_sc[...]
        for s in range(8):
            out = (hloc[s] + ploc[s] * c_in) * jax.nn.gelu(phase(nb, base, s))
            if s == 7:
                out = out + z_out
            for cb in range(nb):
                o_sc[cb, pl.ds(base + s, nr, stride=8), :] = out[:, cb * RNN_BLOCK:(cb + 1) * RNN_BLOCK]
        return c

    xc_a, r_a, i_a = conv_gates(0)
    qT = lax.dot_general(wqT_ref[...], h, nt, preferred_element_type=F32)
    qT_ref[0] = (qT * (HEAD_DIM ** -0.5 * LOG2E)).astype(BF16).reshape(N_HEADS, V_DIM, ts)
    c_mid = recurrence(0, xc_a, r_a, i_a, ch_sc[...], True)

    xc_b, r_b, i_b = conv_gates(half)
    kk = jnp.dot(h, wk_ref[...], preferred_element_type=F32)
    k_ref[0] = kk.astype(BF16)
    vT = lax.dot_general(wvT_ref[...], h, nt, preferred_element_type=F32)
    vT_ref[0, :, 0] = vT.astype(BF16).reshape(N_HEADS, V_DIM, ts)
    ch_sc[...] = recurrence(half, xc_b, r_b, i_b, c_mid, False,
                            z_gate=_zero_row(qT), z_scan=_zero_row(kk), z_out=_zero_row(vT))

    for c in range(nb):
        xg_sc[c, 0:8, :] = xg_sc[c, ts:ts + 8, :]
    o_ref[0] = jnp.concatenate([o_sc[c] for c in range(nb)], axis=1).astype(o_ref.dtype)


def _qkv_rglru(x, g, wqT, wk, wvT, w_xg, conv_w, conv_b, w_ri, b_r, b_i, lru_lambda):
    b, s, d = x.shape
    ts = ATT_TILE
    nb = D_RNN // RNN_BLOCK
    const = lambda shape: pl.BlockSpec(shape, lambda bi, si: (0,) * len(shape))
    return pl.pallas_call(
        _qkv_rglru_kernel,
        grid=(b, s // ts),
        in_specs=[
            pl.BlockSpec((1, ts, d), lambda bi, si: (bi, si, 0)),
            const((1, d)), const((D_QK, d)), const((d, D_QK)), const((D_V, d)),
            const((d, 2 * D_RNN)), const((CONV_WIDTH, D_RNN)), const((1, D_RNN)),
            const((N_RNN_BLOCKS, RNN_BLOCK, 2 * RNN_BLOCK)),
            const((1, D_RNN)), const((1, D_RNN)), const((1, D_RNN)),
        ],
        out_specs=[
            pl.BlockSpec((1, N_HEADS, V_DIM, ts), lambda bi, si: (bi, 0, 0, si)),
            pl.BlockSpec((1, ts, D_QK), lambda bi, si: (bi, si, 0)),
            pl.BlockSpec((1, N_HEADS, 1, V_DIM, ts), lambda bi, si: (bi, 0, si, 0, 0)),
            pl.BlockSpec((1, ts, D_RNN), lambda bi, si: (bi, si, 0)),
        ],
        out_shape=[
            jax.ShapeDtypeStruct((b, N_HEADS, V_DIM, s), BF16),
            jax.ShapeDtypeStruct((b, s, D_QK), BF16),
            jax.ShapeDtypeStruct((b, N_HEADS, s // ts, V_DIM, ts), BF16),
            jax.ShapeDtypeStruct((b, s, D_RNN), BF16),
        ],
        scratch_shapes=[
            pltpu.VMEM((2 * nb, ts + 8, RNN_BLOCK), F32),
            pltpu.VMEM((nb, ts, RNN_BLOCK), F32),
            pltpu.VMEM((1, D_RNN), F32),
            pltpu.VMEM((ts // 16, D_RNN), F32),
            pltpu.VMEM((ts // 16, D_RNN), F32),
            pltpu.VMEM((ts // 16, D_RNN), F32),
        ],
        compiler_params=pltpu.CompilerParams(
            dimension_semantics=("parallel", "arbitrary"), vmem_limit_bytes=VMEM_LIMIT),
        name="qkv_rglru",
    )(x, g, wqT, wk, wvT, w_xg, conv_w, conv_b, w_ri, b_r, b_i, lru_lambda)


def _sublane_allmax(v):
    for sh in (4, 2, 1):
        v = jnp.maximum(v, pltpu.roll(v, sh, 0))
    return v


def _attn_kernel(lamp_ref, qT_ref, qTn_ref, k_ref, vT_ref, kext_ref, qext_ref, mask_ref, slope_ref, gsub_ref,
                 o_ref, qz_sc, s_sc, m_sc, acc_sc):
    g_heads = qT_ref.shape[1]
    tile = qT_ref.shape[3]
    i = pl.program_id(2)
    chains = [(g, c) for g in range(g_heads) for c in range(2)]
    nch = len(chains)
    ones_rows = jnp.ones((ACC_ROWS - V_DIM, tile), BF16)
    cur = jnp.bitwise_and(i, 1)

    def build_qz(src_ref, slot):
        row = lax.broadcasted_iota(jnp.int32, (V_DIM, tile), 0)
        for g in range(g_heads):
            q_both = src_ref[0, g].astype(F32)
            qz_sc[slot, g, 0, 0:V_DIM] = jnp.where(row < HEAD_DIM, q_both, 0.0).astype(BF16)
            qz_sc[slot, g, 1, 0:V_DIM] = jnp.where(row >= HEAD_DIM, q_both, 0.0).astype(BF16)
            qz_sc[slot, g, 0, V_DIM:2 * V_DIM] = qext_ref[g]
            qz_sc[slot, g, 1, V_DIM:2 * V_DIM] = qext_ref[g]

    def scores(slot, j, idx):
        g, c = chains[idx]
        kt = k_ref[0, pl.ds(pl.multiple_of(j * tile, tile), tile), g * V_DIM:(g + 1) * V_DIM]
        lhs = jnp.concatenate([kt, kext_ref[...]], axis=1)
        return jnp.dot(lhs, qz_sc[slot, g, c], preferred_element_type=F32)

    @pl.when(i == 0)
    def _():
        build_qz(qT_ref, 0)
        for idx in range(nch):
            s_sc[idx] = scores(0, 0, idx)

    m_sc[...] = jnp.full(m_sc.shape, NEG_BIG, F32)
    acc_sc[...] = jnp.zeros(acc_sc.shape, F32)

    def update_acc(idx, alpha, pv):
        acc = acc_sc[idx].reshape(ACC_ROWS // 8, 8, tile) * alpha[None] + pv.reshape(ACC_ROWS // 8, 8, tile)
        acc_sc[idx] = acc.reshape(ACC_ROWS, tile)

    def softmax_full(s, m_old, dvec):
        t = s.reshape(tile // 8, 8, tile)
        m_new = jnp.maximum(m_old, _sublane_allmax(jnp.max(t, axis=0)) - dvec)
        return m_new, jnp.exp2(t - (m_new + dvec)[None]).reshape(tile, tile)

    def softmax_diag(s, m_old, dvec):
        nblk = DIAG_BLOCKS
        bs = tile // nblk
        tri = mask_ref[...].reshape(bs // 8, 8, bs)

        def block(kb, qb):
            t = s[kb * bs:(kb + 1) * bs, qb * bs:(qb + 1) * bs].reshape(bs // 8, 8, bs)
            return t + tri if kb == qb else t

        t_blk = {(kb, qb): block(kb, qb) for qb in range(nblk) for kb in range(qb + 1)}
        col_max = []
        for qb in range(nblk):
            m = jnp.max(t_blk[(0, qb)], axis=0)
            for kb in range(1, qb + 1):
                m = jnp.maximum(m, jnp.max(t_blk[(kb, qb)], axis=0))
            col_max.append(m)
        m_new = jnp.maximum(m_old, _sublane_allmax(jnp.concatenate(col_max, axis=1)) - dvec)
        sub = m_new + dvec
        rows = []
        for kb in range(nblk):
            rows.append(jnp.concatenate(
                [jnp.exp2(t_blk[(kb, qb)] - sub[None, :, qb * bs:(qb + 1) * bs]).reshape(bs, bs) if kb <= qb
                 else jnp.zeros((bs, bs), F32) for qb in range(nblk)], axis=1))
        return m_new, jnp.concatenate(rows, axis=0)

    def step(j, diagonal):
        dist = jnp.full((8, tile), i - j, jnp.int32).astype(F32)
        pending = None
        for idx, (g, c) in enumerate(chains):
            s = s_sc[idx]
            s_next = scores(1 - cur, 0, idx) if diagonal else scores(cur, j + 1, idx)
            dvec = slope_ref[g] * dist
            m_old = m_sc[idx]
            m_new, p = (softmax_diag if diagonal else softmax_full)(s, m_old, dvec)
            alpha = jnp.exp2(m_old - m_new)
            m_sc[idx] = m_new
            lhs = jnp.concatenate([vT_ref[0, g, j], ones_rows], axis=0)
            pv = jnp.dot(lhs, p.astype(BF16), preferred_element_type=F32)
            s_sc[idx] = s_next
            if pending is not None:
                update_acc(*pending)
            pending = (idx, alpha, pv)
        update_acc(*pending)

    def body(j, carry):
        step(j, False)
        return carry

    lax.fori_loop(0, i, body, 0)
    build_qz(qTn_ref, 1 - cur)
    step(i, True)

    lp = lamp_ref[...]
    lam = (jnp.exp(jnp.sum(lp[0:1] * lp[1:2], axis=-1, keepdims=True))
           - jnp.exp(jnp.sum(lp[2:3] * lp[3:4], axis=-1, keepdims=True)) + LAM_INIT)
    for g in range(g_heads):
        a1 = acc_sc[2 * g]
        a2 = acc_sc[2 * g + 1]
        inv1 = 1.0 / a1[V_DIM:V_DIM + 1]
        inv2 = lam / a2[V_DIM:V_DIM + 1]
        a = a1[0:V_DIM] * inv1 - a2[0:V_DIM] * inv2
        ms = jnp.mean(a * a, axis=0, keepdims=True)
        y = a * (lax.rsqrt(ms + EPS) * (1.0 - LAM_INIT)) * gsub_ref[...]
        o_ref[0, g] = y.astype(o_ref.dtype)


def _attention(lamp, qT, k, vT5, kext, qext, mask, slope, gsub):
    b, h, _, s = qT.shape
    g = ATT_HEADS_PER_STEP
    t = ATT_TILE
    nq = s // t
    return pl.pallas_call(
        _attn_kernel,
        grid=(b, h // g, nq),
        in_specs=[
            pl.BlockSpec((4, HEAD_DIM), lambda bi, hi, qi: (0, 0)),
            pl.BlockSpec((1, g, V_DIM, t), lambda bi, hi, qi: (bi, hi, 0, qi)),
            pl.BlockSpec((1, g, V_DIM, t), lambda bi, hi, qi: (bi, hi, 0, jnp.minimum(qi + 1, nq - 1))),
            pl.BlockSpec((1, s, g * V_DIM), lambda bi, hi, qi: (bi, 0, hi)),
            pl.BlockSpec((1, g, nq, V_DIM, t), lambda bi, hi, qi: (bi, hi, 0, 0, 0)),
            pl.BlockSpec((t, V_DIM), lambda bi, hi, qi: (0, 0)),
            pl.BlockSpec((g, V_DIM, t), lambda bi, hi, qi: (hi, 0, 0)),
            pl.BlockSpec((t // DIAG_BLOCKS, t // DIAG_BLOCKS), lambda bi, hi, qi: (0, 0)),
            pl.BlockSpec((g, 8, t), lambda bi, hi, qi: (hi, 0, 0)),
            pl.BlockSpec((V_DIM, 1), lambda bi, hi, qi: (0, 0)),
        ],
        out_specs=pl.BlockSpec((1, g, V_DIM, t), lambda bi, hi, qi: (bi, hi, 0, qi)),
        out_shape=jax.ShapeDtypeStruct((b, h, V_DIM, s), BF16),
        scratch_shapes=[
            pltpu.VMEM((2, g, 2, 2 * V_DIM, t), BF16),
            pltpu.VMEM((2 * g, t, t), F32),
            pltpu.VMEM((2 * g, 8, t), F32),
            pltpu.VMEM((2 * g, ACC_ROWS, t), F32),
        ],
        compiler_params=pltpu.CompilerParams(
            dimension_semantics=("parallel", "parallel", "arbitrary"), vmem_limit_bytes=VMEM_LIMIT),
        name="attention",
    )(lamp, qT, qT, k, vT5, kext, qext, mask, slope, gsub)


def _merge_ffn_kernel(x_ref, attT_ref, rnn_ref, gx_ref, wg_ref, bg_ref, wa_ref, wr_ref, wo_ref,
                      gm_ref, w1_ref, w2_ref, gf_ref, o_ref):
    x = x_ref[0]
    h = _rms(x, gx_ref[...]).astype(BF16)
    y_att = lax.dot_general(attT_ref[0], wa_ref[...], (((0,), (0,)), ((), ())),
                            preferred_element_type=F32)
    y_rnn = jnp.dot(rnn_ref[0], wr_ref[...], preferred_element_type=F32)
    gts = jax.nn.sigmoid(jnp.dot(h, wg_ref[...], preferred_element_type=F32) + bg_ref[...])
    m = gts[:, :D_MODEL] * y_att + gts[:, D_MODEL:] * y_rnn
    x1 = x + jnp.dot(m.astype(BF16), wo_ref[...], preferred_element_type=F32)
    h2 = _rms(x1, gm_ref[...]).astype(BF16)
    f = jnp.dot(h2, w1_ref[...], preferred_element_type=F32)
    f = jnp.square(jnp.maximum(f, 0.0)).astype(BF16)
    x2 = x1 + jnp.dot(f, w2_ref[...], preferred_element_type=F32)
    o_ref[0] = _rms(x2, gf_ref[...])


def _merge_ffn(x, attT, rnn, g_mix, w_gate, b_gate, w_att, w_rnn, w_o, g_mlp, w1, w2, g_final):
    b, s, d = x.shape
    tm = FFN_TM
    const = lambda shape: pl.BlockSpec(shape, lambda bi, si: (0,) * len(shape),
                                       pipeline_mode=pl.Buffered(1))
    return pl.pallas_call(
        _merge_ffn_kernel,
        grid=(b, s // tm),
        in_specs=[
            pl.BlockSpec((1, tm, d), lambda bi, si: (bi, si, 0)),
            pl.BlockSpec((1, D_V, tm), lambda bi, si: (bi, 0, si)),
            pl.BlockSpec((1, tm, D_RNN), lambda bi, si: (bi, si, 0)),
            const((1, d)), const((d, 2 * d)),
            const((1, 2 * d)),
            const((D_V, d)), const((D_RNN, d)), const((d, d)),
            const((1, d)),
            const((d, D_FF)), const((D_FF, d)),
            const((1, d)),
        ],
        out_specs=pl.BlockSpec((1, tm, d), lambda bi, si: (bi, si, 0)),
        out_shape=jax.ShapeDtypeStruct((b, s, d), F32),
        compiler_params=pltpu.CompilerParams(
            dimension_semantics=("parallel", "parallel"), vmem_limit_bytes=VMEM_LIMIT),
        name="merge_ffn",
    )(x, attT, rnn, g_mix, w_gate, b_gate, w_att, w_rnn, w_o, g_mlp, w1, w2, g_final)


def _split_bf16(v, pieces=3):
    out, r = [], np.asarray(v, np.float64)
    for _ in range(pieces):
        piece = r.astype(np.float32).astype(BF16)
        out.append(piece)
        r = r - piece.astype(np.float64)
    return out


def _alibi_tables(tile):
    slopes = 2.0 ** (-8.0 * np.arange(1, N_HEADS + 1, dtype=np.float64) / N_HEADS) * LOG2E
    kr = np.arange(tile, dtype=np.float64)
    qr = np.arange(tile, dtype=np.float64)
    assert tile <= 256 * 256
    kext = np.zeros((tile, V_DIM), np.float32)
    kext[:, 0:3] = np.mod(kr, 256.0)[:, None]
    kext[:, 3:6] = np.floor(kr / 256.0)[:, None]
    kext[:, 6:9] = 1.0
    qext = np.zeros((N_HEADS, V_DIM, tile), BF16)
    for n, piece in enumerate(_split_bf16(slopes)):
        qext[:, n, :] = piece[:, None]
        qext[:, 3 + n, :] = (piece.astype(np.float32) * 256.0).astype(BF16)[:, None]
    for n, piece in enumerate(_split_bf16(-slopes[:, None] * qr[None, :])):
        qext[:, 6 + n, :] = piece
    hh = tile // DIAG_BLOCKS
    mask = np.where(kr[:hh, None] > qr[None, :hh], NEG_BIG, 0.0).astype(np.float32)
    slope_rows = np.broadcast_to((slopes * tile).astype(np.float32)[:, None, None], (N_HEADS, 8, tile))
    return jnp.asarray(kext, BF16), jnp.asarray(qext), jnp.asarray(mask), jnp.asarray(slope_rows)


def kernel(x, w_in, b_gate, g_mix, lambda_q1, lambda_k1, lambda_q2, lambda_k2, subln_g,
           conv_w, conv_b, w_r, b_r, w_i, b_i, lru_lambda, w_att_out, w_rnn_out, w_o,
           g_mlp, w_ff1, w_ff2, g_final):
    b, s, d = x.shape
    assert (d, w_in.shape[0]) == (D_MODEL, 1) and s % ATT_TILE == 0
    t = ATT_TILE
    w_in_b = w_in[0].astype(BF16)
    w_q, w_k, w_v = (w_in_b[:, n * D_QK:(n + 1) * D_QK] for n in range(3))
    w_ri = jnp.concatenate([w_r[0], w_i[0]], axis=-1).astype(BF16)
    qT, k, vT5, rnn = _qkv_rglru(x, g_mix, w_q.T, w_k, w_v.T, w_in_b[:, D_QKV:D_QKV + 2 * D_RNN],
                                 conv_w[0], conv_b, w_ri, b_r, b_i, lru_lambda)
    kext, qext, mask, slope_rows = _alibi_tables(t)
    lamp = jnp.concatenate([lambda_q1, lambda_k1, lambda_q2, lambda_k2], axis=0).astype(F32)
    attT = _attention(lamp, qT, k, vT5, kext, qext, mask, slope_rows, subln_g.reshape(V_DIM, 1))
    attT = attT.reshape(b, D_V, s)

    return _merge_ffn(x, attT, rnn, g_mix, w_in_b[:, D_QKV + 2 * D_RNN:], b_gate,
                      w_att_out[0].astype(BF16), w_rnn_out[0].astype(BF16), w_o[0].astype(BF16),
                      g_mlp, w_ff1[0].astype(BF16), w_ff2[0].astype(BF16), g_final.reshape(1, d))
```
